```python
import math
import jax, jax.numpy as jnp
from jax import lax
import numpy as np

D_MODEL = 1024
BATCH = 32
SEQ = 2048
DEPTH = 1

M_HEADS = 4
M_HEAD_DIM = 128
M_WIDTH = M_HEADS * M_HEAD_DIM
M_CONV = 4
M_CHUNK = 64
A_HEADS = 8
A_NOPE = 64
A_ROPE = 32
A_V = 64
A_QK = A_NOPE + A_ROPE
A_Q_RANK = 256
A_KV_RANK = 128
A_WIDTH = A_HEADS * A_V
ROPE_THETA = 10000.0
Q_BLOCK = 128
D_FF = ((8 * D_MODEL + 3 * 256 - 1) // (3 * 256)) * 256
NORM_EPS = 1e-6
IN_SIZES = (M_WIDTH, M_WIDTH, M_WIDTH, M_WIDTH, M_HEADS, M_HEADS,
            A_Q_RANK, A_KV_RANK, A_ROPE, D_MODEL, D_MODEL)
D_IN = sum(IN_SIZES)

kernel_name = "hybrid_mlstm_mla_gated_block"


def rmsnorm(x, g):
    xf = x.astype(jnp.float32)
    y = xf * lax.rsqrt(jnp.mean(xf * xf, axis=-1, keepdims=True) + NORM_EPS)
    return (y * g.astype(jnp.float32)).astype(x.dtype)


def split_cols(z, sizes):
    idx = [int(v) for v in np.cumsum(sizes)[:-1]]
    return jnp.split(z, idx, axis=-1)


def rope_tables(positions):
    inv_freq = ROPE_THETA ** (-jnp.arange(0, A_ROPE, 2, dtype=jnp.float32) / A_ROPE)
    ang = positions.astype(jnp.float32)[..., None] * inv_freq
    return jnp.cos(ang), jnp.sin(ang)


def apply_rope(x, cos, sin):
    x1, x2 = jnp.split(x.astype(jnp.float32), 2, axis=-1)
    out = jnp.concatenate([x1 * cos - x2 * sin, x2 * cos + x1 * sin], axis=-1)
    return out.astype(x.dtype)


def causal_conv(x, w, b):
    S = x.shape[1]
    xp = jnp.pad(x, ((0, 0), (M_CONV - 1, 0), (0, 0)))
    y = b
    for j in range(M_CONV):
        y = y + xp[:, j:j + S] * w[j]
    return y


def to_chunks(t):
    B, S, H = t.shape[:3]
    t = t.reshape((B, S // M_CHUNK, M_CHUNK, H) + t.shape[3:])
    return jnp.moveaxis(t, (1, 3), (0, 2))


def mlstm_chunkwise(q, k, v, ig, lf):
    B, S, H, d = q.shape
    qc, kc, vc = to_chunks(q), to_chunks(k), to_chunks(v)
    igc, lfc = to_chunks(ig), to_chunks(lf)
    causal = jnp.tril(jnp.ones((M_CHUNK, M_CHUNK), dtype=bool))

    def step(carry, inp):
        C, n, m = carry
        qq, kk, vv, ii, ff = inp
        b = jnp.cumsum(ff, axis=-1)
        dlog = b[..., :, None] - b[..., None, :] + ii[..., None, :]
        dlog = jnp.where(causal, dlog, -jnp.inf)
        inter_log = b + m[..., None]
        m_t = jnp.maximum(inter_log, jnp.max(dlog, axis=-1))
        dw = jnp.exp(dlog - m_t[..., None])
        inter_w = jnp.exp(inter_log - m_t)
        s = jnp.einsum('bhtd,bhsd->bhts', qq, kk) * dw
        num = jnp.einsum('bhts,bhse->bhte', s, vv) + inter_w[..., None] * jnp.einsum('bhtd,bhde->bhte', qq, C)
        den = jnp.sum(s, axis=-1) + inter_w * jnp.einsum('bhtd,bhd->bht', qq, n)
        h = num / jnp.maximum(jnp.abs(den), jnp.exp(-m_t))[..., None]
        b_end = b[..., -1]
        wlog = b_end[..., None] - b + ii
        m_new = jnp.maximum(b_end + m, jnp.max(wlog, axis=-1))
        w = jnp.exp(wlog - m_new[..., None])
        decay = jnp.exp(b_end + m - m_new)
        C_new = decay[..., None, None] * C + jnp.einsum('bhs,bhsd,bhse->bhde', w, kk, vv)
        n_new = decay[..., None] * n + jnp.einsum('bhs,bhsd->bhd', w, kk)
        return (C_new, n_new, m_new), h

    init = (jnp.zeros((B, H, d, d), jnp.float32), jnp.zeros((B, H, d), jnp.float32),
            jnp.zeros((B, H), jnp.float32))
    _, hc = lax.scan(step, init, (qc, kc, vc, igc, lfc))
    return jnp.moveaxis(hc, (0, 2), (1, 3)).reshape(B, S, H, d)


def mla_attention(q_nope, q_rope, k_nope, k_rope, v):
    S = q_nope.shape[1]
    scale = A_QK ** -0.5
    outs = []
    for start in range(0, S, Q_BLOCK):
        end = start + Q_BLOCK
        s = (jnp.einsum('bqhd,bkhd->bhqk', q_nope[:, start:end], k_nope[:, :end])
             + jnp.einsum('bqhr,bkr->bhqk', q_rope[:, start:end], k_rope[:, :end]))
        s = s.astype(jnp.float32) * scale
        qi = start + jnp.arange(Q_BLOCK)[:, None]
        ki = jnp.arange(end)[None, :]
        s = jnp.where(ki <= qi, s, -jnp.inf)
        p = jax.nn.softmax(s, axis=-1).astype(v.dtype)
        outs.append(jnp.einsum('bhqk,bkhd->bqhd', p, v[:, :end]))
    return jnp.concatenate(outs, axis=1)


def setup_inputs(seed: int = 0) -> dict:
    key = jax.random.key(seed)
    ks = jax.random.split(key, 24)
    f32 = jnp.float32
    nrm = lambda k, shape, s: jax.random.normal(k, shape, f32) * s
    gain = lambda k, shape: 1.0 + 0.02 * jax.random.normal(k, shape, f32)
    L = DEPTH
    x = jax.random.normal(ks[0], (BATCH, SEQ, D_MODEL), f32)
    c = jax.random.normal(ks[1], (BATCH, D_MODEL), f32)
    offs = jax.random.randint(ks[2], (BATCH, 1), 0, 1024, dtype=jnp.int32)
    positions = (offs + jnp.arange(SEQ, dtype=jnp.int32)[None, :]).astype(jnp.int32)
    return {
        "x": x,
        "c": c,
        "positions": positions,
        "w_ada": nrm(ks[3], (L, D_MODEL, 6 * D_MODEL), 0.5 * D_MODEL ** -0.5),
        "b_ada": nrm(ks[4], (L, 6 * D_MODEL), 0.02),
        "g_mix": gain(ks[5], (L, D_MODEL)),
        "w_in": nrm(ks[6], (L, D_MODEL, D_IN), D_MODEL ** -0.5),
        "conv_w": nrm(ks[7], (L, M_CONV, 2 * M_WIDTH), M_CONV ** -0.5),
        "conv_b": nrm(ks[8], (L, 2 * M_WIDTH), 0.02),
        "b_igate": nrm(ks[9], (L, M_HEADS), 0.1),
        "b_fgate": jnp.linspace(3.0, 6.0, M_HEADS, dtype=f32)[None, :] + nrm(ks[10], (L, M_HEADS), 0.1),
        "g_mhead": gain(ks[11], (L, M_WIDTH)),
        "g_q_lat": gain(ks[12], (L, A_Q_RANK)),
        "w_uq": nrm(ks[13], (L, A_Q_RANK, A_HEADS * A_QK), A_Q_RANK ** -0.5),
        "g_kv_lat": gain(ks[14], (L, A_KV_RANK)),
        "w_ukv": nrm(ks[15], (L, A_KV_RANK, A_HEADS * (A_NOPE + A_V)), A_KV_RANK ** -0.5),
        "w_branch_m": nrm(ks[16], (L, M_WIDTH, D_MODEL), M_WIDTH ** -0.5),
        "w_branch_a": nrm(ks[17], (L, A_WIDTH, D_MODEL), A_WIDTH ** -0.5),
        "w_out": nrm(ks[18], (L, D_MODEL, D_MODEL), D_MODEL ** -0.5),
        "g_ffn": gain(ks[19], (L, D_MODEL)),
        "w_ffn_in": nrm(ks[20], (L, D_MODEL, 2 * D_FF), D_MODEL ** -0.5),
        "w_ffn_out": nrm(ks[21], (L, D_FF, D_MODEL), D_FF ** -0.5),
        "g_final": gain(ks[22], (D_MODEL,)),
    }


def reference(x, c, positions, w_ada, b_ada, g_mix, w_in, conv_w, conv_b, b_igate, b_fgate,
              g_mhead, g_q_lat, w_uq, g_kv_lat, w_ukv, w_branch_m, w_branch_a, w_out,
              g_ffn, w_ffn_in, w_ffn_out, g_final):
    B, S, _ = x.shape
    cos, sin = rope_tables(positions)
    cos_q, sin_q = cos[:, :, None, :], sin[:, :, None, :]
    h = x
    for l in range(DEPTH):
        ada = jax.nn.silu(c) @ w_ada[l] + b_ada[l]
        sh1, sc1, gt1, sh2, sc2, gt2 = [a[:, None, :] for a in jnp.split(ada, 6, axis=-1)]

        u = rmsnorm(h, g_mix[l]) * (1.0 + sc1) + sh1
        z = u @ w_in[l]
        (qm, km, vm, om, ipre, fpre, q_lat, kv_lat, kr_raw,
         gate_m, gate_a) = split_cols(z, IN_SIZES)

        qk = jax.nn.silu(causal_conv(jnp.concatenate([qm, km], axis=-1), conv_w[l], conv_b[l]))
        qm, km = jnp.split(qk, 2, axis=-1)
        f32 = jnp.float32
        qh = qm.reshape(B, S, M_HEADS, M_HEAD_DIM).astype(f32) * (M_HEAD_DIM ** -0.5)
        kh = km.reshape(B, S, M_HEADS, M_HEAD_DIM).astype(f32)
        vh = vm.reshape(B, S, M_HEADS, M_HEAD_DIM).astype(f32)
        log_i = ipre.astype(f32) + b_igate[l].astype(f32)
        log_f = jax.nn.log_sigmoid(fpre.astype(f32) + b_fgate[l].astype(f32))
        hm = mlstm_chunkwise(qh, kh, vh, log_i, log_f)
        hm = hm * lax.rsqrt(jnp.mean(hm * hm, axis=-1, keepdims=True) + NORM_EPS)
        hm = hm * g_mhead[l].astype(f32).reshape(M_HEADS, M_HEAD_DIM)
        hm = (jax.nn.sigmoid(om.astype(f32)) * hm.reshape(B, S, M_WIDTH)).astype(x.dtype)
        y_m = hm @ w_branch_m[l]

        cq = rmsnorm(q_lat, g_q_lat[l])
        qa = (cq @ w_uq[l]).reshape(B, S, A_HEADS, A_QK)
        q_nope, q_rope = qa[..., :A_NOPE], apply_rope(qa[..., A_NOPE:], cos_q, sin_q)
        ckv = rmsnorm(kv_lat, g_kv_lat[l])
        kva = (ckv @ w_ukv[l]).reshape(B, S, A_HEADS, A_NOPE + A_V)
        k_nope, va = kva[..., :A_NOPE], kva[..., A_NOPE:]
        k_rope = apply_rope(kr_raw, cos, sin)
        oa = mla_attention(q_nope, q_rope, k_nope, k_rope, va).reshape(B, S, A_WIDTH)
        y_a = oa @ w_branch_a[l]

        y = jax.nn.sigmoid(gate_m) * y_m + jax.nn.sigmoid(gate_a) * y_a
        h = h + gt1 * (y @ w_out[l])

        u2 = rmsnorm(h, g_ffn[l]) * (1.0 + sc2) + sh2
        gu = u2 @ w_ffn_in[l]
        g_, up = gu[..., :D_FF], gu[..., D_FF:]
        h = h + gt2 * ((jax.nn.silu(g_) * up) @ w_ffn_out[l])
    return rmsnorm(h, g_final)
```

```python
import functools

import jax
import jax.numpy as jnp
import numpy as np
from jax import lax
from jax.experimental import pallas as pl
from jax.experimental.pallas import tpu as pltpu

F32 = jnp.float32
BF16 = jnp.bfloat16

D_MODEL = 1024
M_HEADS = 4
M_HEAD_DIM = 128
M_WIDTH = M_HEADS * M_HEAD_DIM
M_CONV = 4
A_HEADS = 8
A_NOPE = 64
A_ROPE = 32
A_V = 64
A_QK = A_NOPE + A_ROPE
A_Q_RANK = 256
A_KV_RANK = 128
A_WIDTH = A_HEADS * A_V
ROPE_THETA = 10000.0
D_FF = 2816
NORM_EPS = 1e-6

LANES = 128
HALO = 8
VMEM_LIMIT = 48 * 1024 * 1024

MIX_TM = 512
M_CHUNK = 128
ATT_TQ = 256
ATT_TK = 256
FFN_TM = 512
FFN_TF = 1408


def _sigmoid(x):
    return 1.0 / (1.0 + jnp.exp(-x))


def _silu(x):
    return x * _sigmoid(x)


def _log_sigmoid(x):
    return jnp.minimum(x, 0.0) - jnp.log1p(jnp.exp(-jnp.abs(x)))


def _rms(x, g):
    ms = jnp.mean(x * x, axis=-1, keepdims=True)
    return x * lax.rsqrt(ms + NORM_EPS) * g


def _bdot(a, b):
    return jnp.dot(a, b, preferred_element_type=F32)


def _dot_nt(a, b):
    return lax.dot_general(a, b, (((1,), (1,)), ((), ())), preferred_element_type=F32)


def _ada_kernel(c_ref, w_ref, b_ref, o_ref):
    c = c_ref[...]
    o_ref[...] = jnp.dot(_silu(c), w_ref[...], preferred_element_type=F32,
                         precision=lax.Precision.HIGHEST) + b_ref[...]


def _ada(c, w_ada, b_ada):
    B, D = c.shape
    N = w_ada.shape[1]
    return pl.pallas_call(
        _ada_kernel,
        grid=(N // D,),
        in_specs=[pl.BlockSpec((B, D), lambda j: (0, 0)),
                  pl.BlockSpec((D, D), lambda j: (0, j)),
                  pl.BlockSpec((1, D), lambda j: (0, j))],
        out_specs=pl.BlockSpec((B, D), lambda j: (0, j)),
        out_shape=jax.ShapeDtypeStruct((B, N), F32),
        compiler_params=pltpu.CompilerParams(dimension_semantics=("arbitrary",),
                                             vmem_limit_bytes=VMEM_LIMIT),
        name="ada",
    )(c, w_ada, b_ada.reshape(1, N))


def _rope_partner(blk, lane):
    up = pltpu.roll(blk, LANES - A_ROPE // 2, 1)
    dn = pltpu.roll(blk, A_ROPE // 2, 1)
    return jnp.where(lane < A_NOPE + A_ROPE // 2, up, dn)


def _mix_in_kernel(x_ref, xh_ref, ada_ref, pos_ref, gmix_ref, w_ref, wgt_ref, cw_ref, cb_ref,
                   bcol_ref, gq_ref, gkv_ref, wuq_ref, wk_ref, wv_ref, invf_ref, sgn_ref,
                   qm_ref, km_ref, vm_ref, om_ref, gr_ref, qa_ref, ka_ref, va_ref):
    i = pl.program_id(1)
    tm = x_ref.shape[1]
    sh1 = ada_ref[0, 0:1, :]
    sc1 = ada_ref[0, 1:2, :]
    g = gmix_ref[...]

    def modulate(xv):
        return _rms(xv, g) * (1.0 + sc1) + sh1

    u = modulate(x_ref[0]).astype(BF16)
    uh = modulate(xh_ref[0]).astype(BF16)

    wqk = w_ref[:, 0:2 * M_WIDTH]
    z = _bdot(u, wqk)
    zh = _bdot(uh, wqk)
    zh = jnp.where(i > 0, zh, 0.0)
    ext = jnp.concatenate([zh, z], axis=0)
    y = cb_ref[...]
    for j in range(M_CONV):
        sft = M_CONV - 1 - j
        y = y + ext[HALO - sft:HALO - sft + tm] * cw_ref[j:j + 1, :]
    qk = _silu(y)
    qm_ref[0] = (qk[:, :M_WIDTH] * (M_HEAD_DIM ** -0.5)).astype(BF16)
    km_ref[0] = qk[:, M_WIDTH:].astype(BF16)

    vm_ref[0] = _bdot(u, w_ref[:, 2 * M_WIDTH:3 * M_WIDTH]).astype(BF16)
    om_ref[0] = _bdot(u, w_ref[:, 3 * M_WIDTH:4 * M_WIDTH])

    gpre = _dot_nt(wgt_ref[...], u) + bcol_ref[...]
    row = lax.broadcasted_iota(jnp.int32, gpre.shape, 0)
    gr_ref[0] = jnp.where(row < M_HEADS, gpre, _log_sigmoid(gpre))

    lat = _bdot(u, w_ref[:, 4 * M_WIDTH:4 * M_WIDTH + 4 * LANES])
    q_lat = lat[:, 0:A_Q_RANK]
    kv_lat = lat[:, A_Q_RANK:A_Q_RANK + A_KV_RANK]
    misc = lat[:, A_Q_RANK + A_KV_RANK:]

    pos = pos_ref[0].astype(F32)
    ang = pos * invf_ref[...]
    cosv = jnp.cos(ang)
    sinv = jnp.sin(ang) * sgn_ref[...]
    lane = lax.broadcasted_iota(jnp.int32, (tm, LANES), 1)

    cq = _rms(q_lat, gq_ref[...]).astype(BF16)
    qa = _bdot(cq, wuq_ref[...])
    scale = A_QK ** -0.5
    for h in range(A_HEADS):
        blk = qa[:, h * LANES:(h + 1) * LANES]
        rot = blk * cosv + _rope_partner(blk, lane) * sinv
        qa_ref[0, :, h * LANES:(h + 1) * LANES] = (rot * scale).astype(BF16)

    ckv = _rms(kv_lat, gkv_ref[...]).astype(BF16)
    kn = _bdot(ckv, wk_ref[...])
    va_ref[0] = _bdot(ckv, wv_ref[...]).astype(BF16)
    is_rope = (lane >= A_NOPE) & (lane < A_QK)
    krot = jnp.where(is_rope, misc * cosv + _rope_partner(misc, lane) * sinv, 0.0)
    for h in range(A_HEADS):
        ka_ref[0, :, h * LANES:(h + 1) * LANES] = (kn[:, h * LANES:(h + 1) * LANES] + krot).astype(BF16)


def _mix_in(x, ada3, pos3, g_mix, w_b, w_gt, conv_w, conv_b, b_col, g_q, g_kv, wuq, wk, wv,
            invf, sgn):
    B, S, D = x.shape
    tm = min(MIX_TM, S)
    nt = S // tm
    hb = tm // HALO
    const = lambda shape: pl.BlockSpec(shape, lambda b, i: (0,) * len(shape))
    tok = lambda w: pl.BlockSpec((1, tm, w), lambda b, i: (b, i, 0))
    out_shapes = (
        jax.ShapeDtypeStruct((B, S, M_WIDTH), BF16),
        jax.ShapeDtypeStruct((B, S, M_WIDTH), BF16),
        jax.ShapeDtypeStruct((B, S, M_WIDTH), BF16),
        jax.ShapeDtypeStruct((B, S, M_WIDTH), F32),
        jax.ShapeDtypeStruct((B, 2 * M_HEADS, S), F32),
        jax.ShapeDtypeStruct((B, S, A_HEADS * LANES), BF16),
        jax.ShapeDtypeStruct((B, S, A_HEADS * LANES), BF16),
        jax.ShapeDtypeStruct((B, S, A_WIDTH), BF16),
    )
    return pl.pallas_call(
        _mix_in_kernel,
        grid=(B, nt),
        in_specs=[tok(D),
                  pl.BlockSpec((1, HALO, D), lambda b, i: (b, jnp.maximum(i * hb - 1, 0), 0)),
                  pl.BlockSpec((1, 6, D), lambda b, i: (b, 0, 0)),
                  tok(1),
                  const((1, D)), const(w_b.shape), const(w_gt.shape), const(conv_w.shape),
                  const(conv_b.shape), const(b_col.shape), const(g_q.shape), const(g_kv.shape),
                  const(wuq.shape), const(wk.shape), const(wv.shape), const(invf.shape),
                  const(sgn.shape)],
        out_specs=(tok(M_WIDTH), tok(M_WIDTH), tok(M_WIDTH), tok(M_WIDTH),
                   pl.BlockSpec((1, 2 * M_HEADS, tm), lambda b, i: (b, 0, i)),
                   tok(A_HEADS * LANES), tok(A_HEADS * LANES), tok(A_WIDTH)),
        out_shape=out_shapes,
        compiler_params=pltpu.CompilerParams(dimension_semantics=("parallel", "arbitrary"),
                                             vmem_limit_bytes=VMEM_LIMIT),
        name="mix_in",
    )(x, x, ada3, pos3, g_mix, w_b, w_gt, conv_w, conv_b, b_col, g_q, g_kv, wuq, wk, wv, invf, sgn)


def _cumsum_lanes(x):
    lane = lax.broadcasted_iota(jnp.int32, x.shape, 1)
    sh = 1
    while sh < x.shape[1]:
        x = x + jnp.where(lane >= sh, pltpu.roll(x, sh, 1), 0.0)
        sh *= 2
    return x


def _mlstm_kernel(q_ref, k_ref, v_ref, om_ref, gr_ref, gmh_ref, o_ref, c_scr, m_scr):
    S = q_ref.shape[1]
    L = M_CHUNK
    dh = M_HEAD_DIM
    c_scr[...] = jnp.zeros_like(c_scr)
    m_scr[...] = jnp.zeros_like(m_scr)
    rows = lax.broadcasted_iota(jnp.int32, (L, L), 0)
    cols = lax.broadcasted_iota(jnp.int32, (L, L), 1)
    causal = rows >= cols
    one_col = (lax.broadcasted_iota(jnp.int32, (L, dh), 1) == 0).astype(BF16)
    grow = lax.broadcasted_iota(jnp.int32, (2 * M_HEADS, L), 0)

    def chunk(c, carry):
        off = pl.multiple_of(c * L, L)
        gr = gr_ref[0, :, pl.ds(off, L)]
        cs = _cumsum_lanes(gr)
        rowv = jnp.where(grow < M_HEADS, gr, cs)
        colv = jnp.concatenate([rowv] * (L // (2 * M_HEADS)), axis=0).T
        for h in range(M_HEADS):
            hs = slice(h * dh, (h + 1) * dh)
            q = q_ref[0, pl.ds(off, L), hs]
            k = k_ref[0, pl.ds(off, L), hs]
            v = v_ref[0, pl.ds(off, L), hs]
            li_row = rowv[h:h + 1, :]
            b_row = rowv[M_HEADS + h:M_HEADS + h + 1, :]
            li_col = colv[:, h:h + 1]
            b_col = colv[:, M_HEADS + h:M_HEADS + h + 1]
            m_prev = m_scr[h:h + 1, 0:1]
            c_aug = c_scr[h]

            dlog = jnp.where(causal, (b_col - b_row) + li_row, -jnp.inf)
            inter_log = b_col + m_prev
            m_t = jnp.maximum(inter_log, jnp.max(dlog, axis=-1, keepdims=True))
            dw = jnp.exp(dlog - m_t)
            inter_w = jnp.exp(inter_log - m_t)
            s = _dot_nt(q, k) * dw
            qc = _bdot(q, c_aug.astype(BF16))
            num = _bdot(s.astype(BF16), v) + inter_w * qc[:, :dh]
            den = jnp.sum(s, axis=-1, keepdims=True) + inter_w * qc[:, dh:dh + 1]
            hv = num / jnp.maximum(jnp.abs(den), jnp.exp(-m_t))
            hn = _rms(hv, gmh_ref[:, hs])
            o_ref[0, pl.ds(off, L), hs] = (_sigmoid(om_ref[0, pl.ds(off, L), hs]) * hn).astype(o_ref.dtype)

            b_end = b_row[:, L - 1:L]
            wlog_row = (b_end - b_row) + li_row
            m_new = jnp.maximum(b_end + m_prev, jnp.max(wlog_row, axis=-1, keepdims=True))
            w_col = jnp.exp((b_end - b_col) + li_col - m_new)
            decay = jnp.exp(b_end + m_prev - m_new)
            kw_t = (k.astype(F32) * w_col).T.astype(BF16)
            v_aug = jnp.concatenate([v, one_col], axis=1)
            c_scr[h] = decay * c_aug + _bdot(kw_t, v_aug)
            m_scr[h:h + 1, :] = jnp.broadcast_to(m_new, (1, LANES))
        return carry

    lax.fori_loop(0, S // L, chunk, 0)


def _mlstm(qm, km, vm, om, gr, g_mhead):
    B, S, W = qm.shape
    seq = lambda: pl.BlockSpec((1, S, W), lambda b: (b, 0, 0))
    return pl.pallas_call(
        _mlstm_kernel,
        grid=(B,),
        in_specs=[seq(), seq(), seq(), seq(),
                  pl.BlockSpec((1, 2 * M_HEADS, S), lambda b: (b, 0, 0)),
                  pl.BlockSpec((1, W), lambda b: (0, 0))],
        out_specs=seq(),
        out_shape=jax.ShapeDtypeStruct((B, S, W), BF16),
        scratch_shapes=[pltpu.VMEM((M_HEADS, M_HEAD_DIM, 2 * M_HEAD_DIM), F32),
                        pltpu.VMEM((2 * M_HEADS, LANES), F32)],
        compiler_params=pltpu.CompilerParams(dimension_semantics=("parallel",),
                                             vmem_limit_bytes=VMEM_LIMIT),
        name="mlstm",
    )(qm, km, vm, om, gr, g_mhead)


def _attn_kernel(q_ref, k_ref, v_ref, o_ref):
    qi = pl.program_id(2)
    tq = q_ref.shape[1]
    tk = ATT_TK if k_ref.shape[1] >= ATT_TK else k_ref.shape[1]
    kt_per_q = tq // tk
    rows = lax.broadcasted_iota(jnp.int32, (tq, tk), 0) + qi * tq
    cols = lax.broadcasted_iota(jnp.int32, (tq, tk), 1)
    outs = []
    for hh in range(2):
        q = q_ref[0, :, hh * LANES:(hh + 1) * LANES]

        def step(kt, carry):
            m, l, acc = carry
            off = pl.multiple_of(kt * tk, tk)
            k = k_ref[0, pl.ds(off, tk), hh * LANES:(hh + 1) * LANES]
            v = v_ref[0, pl.ds(off, tk), :]
            s = _dot_nt(q, k)
            s = jnp.where(cols + off <= rows, s, -jnp.inf)
            m_new = jnp.maximum(m, jnp.max(s, axis=-1, keepdims=True))
            alpha = jnp.exp(m - m_new)
            p = jnp.exp(s - m_new)
            l = alpha * l + jnp.sum(p, axis=-1, keepdims=True)
            acc = alpha * acc + _bdot(p.astype(BF16), v)
            return m_new, l, acc

        init = (jnp.full((tq, 1), -jnp.inf, F32), jnp.zeros((tq, 1), F32),
                jnp.zeros((tq, LANES), F32))
        m, l, acc = lax.fori_loop(0, (qi + 1) * kt_per_q, step, init)
        outs.append(acc / l)
    lane = lax.broadcasted_iota(jnp.int32, (tq, LANES), 1)
    o_ref[0] = jnp.where(lane < A_V, outs[0], outs[1]).astype(o_ref.dtype)


def _attn(qa, ka, va):
    B, S, _ = qa.shape
    tq = min(ATT_TQ, S)
    return pl.pallas_call(
        _attn_kernel,
        grid=(B, A_HEADS // 2, S // tq),
        in_specs=[pl.BlockSpec((1, tq, 2 * LANES), lambda b, p, i: (b, i, p)),
                  pl.BlockSpec((1, S, 2 * LANES), lambda b, p, i: (b, 0, p)),
                  pl.BlockSpec((1, S, LANES), lambda b, p, i: (b, 0, p))],
        out_specs=pl.BlockSpec((1, tq, LANES), lambda b, p, i: (b, i, p)),
        out_shape=jax.ShapeDtypeStruct((B, S, A_WIDTH), BF16),
        compiler_params=pltpu.CompilerParams(
            dimension_semantics=("parallel", "parallel", "arbitrary"),
            vmem_limit_bytes=VMEM_LIMIT),
        name="attn",
    )(qa, ka, va)


def _merge_kernel(x_ref, ada_ref, hm_ref, oa_ref, gmix_ref, wg_ref, wbm_ref, wba_ref, wo_ref, h_ref):
    x = x_ref[0]
    sh1 = ada_ref[0, 0:1, :]
    sc1 = ada_ref[0, 1:2, :]
    gt1 = ada_ref[0, 2:3, :]
    u = (_rms(x, gmix_ref[...]) * (1.0 + sc1) + sh1).astype(BF16)
    D = x.shape[1]
    y = _sigmoid(_bdot(u, wg_ref[:, 0:D])) * _bdot(hm_ref[0], wbm_ref[...])
    y = y + _sigmoid(_bdot(u, wg_ref[:, D:2 * D])) * _bdot(oa_ref[0], wba_ref[...])
    h_ref[0] = x + gt1 * _bdot(y.astype(BF16), wo_ref[...])


def _merge(x, ada3, hm, oa, g_mix, w_g, w_bm, w_ba, w_o):
    B, S, D = x.shape
    tm = min(MIX_TM, S)
    const = lambda shape: pl.BlockSpec(shape, lambda b, i: (0,) * len(shape))
    tok = lambda w: pl.BlockSpec((1, tm, w), lambda b, i: (b, i, 0))
    return pl.pallas_call(
        _merge_kernel,
        grid=(B, S // tm),
        in_specs=[tok(D), pl.BlockSpec((1, 6, D), lambda b, i: (b, 0, 0)), tok(M_WIDTH), tok(A_WIDTH),
                  const((1, D)), const(w_g.shape), const(w_bm.shape), const(w_ba.shape),
                  const(w_o.shape)],
        out_specs=tok(D),
        out_shape=jax.ShapeDtypeStruct((B, S, D), F32),
        compiler_params=pltpu.CompilerParams(dimension_semantics=("parallel", "parallel"),
                                             vmem_limit_bytes=VMEM_LIMIT),
        name="merge",
    )(x, ada3, hm, oa, g_mix, w_g, w_bm, w_ba, w_o)


def _ffn_kernel(h_ref, ada_ref, gffn_ref, gfin_ref, wg_ref, wu_ref, wo_ref, o_ref, u_scr, acc_scr):
    j = pl.program_id(2)

    @pl.when(j == 0)
    def _():
        sh2 = ada_ref[0, 3:4, :]
        sc2 = ada_ref[0, 4:5, :]
        u_scr[...] = (_rms(h_ref[0], gffn_ref[...]) * (1.0 + sc2) + sh2).astype(BF16)
        acc_scr[...] = jnp.zeros_like(acc_scr)

    u = u_scr[...]
    a = _silu(_bdot(u, wg_ref[...])) * _bdot(u, wu_ref[...])
    acc_scr[...] += _bdot(a.astype(BF16), wo_ref[...])

    @pl.when(j == pl.num_programs(2) - 1)
    def _():
        gt2 = ada_ref[0, 5:6, :]
        h2 = h_ref[0] + gt2 * acc_scr[...]
        o_ref[0] = _rms(h2, gfin_ref[...])


def _ffn(h1, ada3, g_ffn, g_final, w_in, w_out):
    B, S, D = h1.shape
    tm = min(FFN_TM, S)
    nf = D_FF // FFN_TF
    tok = pl.BlockSpec((1, tm, D), lambda b, i, j: (b, i, 0))
    return pl.pallas_call(
        _ffn_kernel,
        grid=(B, S // tm, nf),
        in_specs=[tok,
                  pl.BlockSpec((1, 6, D), lambda b, i, j: (b, 0, 0)),
                  pl.BlockSpec((1, D), lambda b, i, j: (0, 0)),
                  pl.BlockSpec((1, D), lambda b, i, j: (0, 0)),
                  pl.BlockSpec((D, FFN_TF), lambda b, i, j: (0, j)),
                  pl.BlockSpec((D, FFN_TF), lambda b, i, j: (0, j + nf)),
                  pl.BlockSpec((FFN_TF, D), lambda b, i, j: (j, 0))],
        out_specs=tok,
        out_shape=jax.ShapeDtypeStruct((B, S, D), F32),
        scratch_shapes=[pltpu.VMEM((tm, D), BF16), pltpu.VMEM((tm, D), F32)],
        compiler_params=pltpu.CompilerParams(
            dimension_semantics=("parallel", "parallel", "arbitrary"),
            vmem_limit_bytes=VMEM_LIMIT),
        name="ffn",
    )(h1, ada3, g_ffn, g_final, w_in, w_in, w_out)


def _pack_weights(w_in, w_uq, w_ukv):
    D = w_in.shape[0]
    o_gate = 4 * M_WIDTH
    o_qlat = o_gate + 2 * M_HEADS
    o_kr = o_qlat + A_Q_RANK + A_KV_RANK
    o_gm = o_kr + A_ROPE
    zeros = lambda n: jnp.zeros((D, n), w_in.dtype)
    misc = jnp.concatenate([zeros(A_NOPE), w_in[:, o_kr:o_gm], zeros(LANES - A_QK)], axis=1)
    w_b = jnp.concatenate([w_in[:, :o_gate], w_in[:, o_qlat:o_kr], misc], axis=1).astype(BF16)
    w_gt = w_in[:, o_gate:o_qlat].T.astype(BF16)
    w_g = w_in[:, o_gm:].astype(BF16)
    wuq = jnp.pad(w_uq.reshape(A_Q_RANK, A_HEADS, A_QK), ((0, 0), (0, 0), (0, LANES - A_QK)))
    wuq = wuq.reshape(A_Q_RANK, A_HEADS * LANES).astype(BF16)
    wkv = w_ukv.reshape(A_KV_RANK, A_HEADS, A_NOPE + A_V)
    wk = jnp.pad(wkv[:, :, :A_NOPE], ((0, 0), (0, 0), (0, LANES - A_NOPE)))
    wk = wk.reshape(A_KV_RANK, A_HEADS * LANES).astype(BF16)
    wv = wkv[:, :, A_NOPE:].reshape(A_KV_RANK, A_WIDTH).astype(BF16)
    return w_b, w_gt, w_g, wuq, wk, wv


def _rope_rows():
    inv_freq = ROPE_THETA ** (-np.arange(0, A_ROPE, 2, dtype=np.float32) / A_ROPE)
    half = A_ROPE // 2
    invf = np.zeros((1, LANES), np.float32)
    invf[0, A_NOPE:A_NOPE + half] = inv_freq
    invf[0, A_NOPE + half:A_QK] = inv_freq
    sgn = np.zeros((1, LANES), np.float32)
    sgn[0, A_NOPE:A_NOPE + half] = -1.0
    sgn[0, A_NOPE + half:A_QK] = 1.0
    return invf, sgn


def kernel(x, c, positions, w_ada, b_ada, g_mix, w_in, conv_w, conv_b, b_igate, b_fgate, g_mhead,
           g_q_lat, w_uq, g_kv_lat, w_ukv, w_branch_m, w_branch_a, w_out, g_ffn, w_ffn_in,
           w_ffn_out, g_final):
    B, S, D = x.shape
    depth = w_ada.shape[0]
    inv_freq = ROPE_THETA ** (-jnp.arange(0, A_ROPE, 2, dtype=F32) / A_ROPE)
    zpad = lambda n: jnp.zeros((n,), F32)
    invf = jnp.concatenate([zpad(A_NOPE), inv_freq, inv_freq, zpad(LANES - A_QK)]).reshape(1, LANES)
    half = A_ROPE // 2
    sgn = jnp.concatenate([zpad(A_NOPE), -jnp.ones((half,), F32), jnp.ones((half,), F32),
                           zpad(LANES - A_QK)]).reshape(1, LANES)
    pos3 = positions.reshape(B, S, 1)
    h = x
    for l in range(depth):
        ada3 = _ada(c, w_ada[l], b_ada[l]).reshape(B, 6, D)
        w_b, w_gt, w_g, wuq, wk, wv = _pack_weights(w_in[l], w_uq[l], w_ukv[l])
        b_col = jnp.concatenate([b_igate[l], b_fgate[l]]).astype(F32).reshape(2 * M_HEADS, 1)
        gmix = g_mix[l].reshape(1, D)
        qm, km, vm, om, gr, qa, ka, va = _mix_in(
            h, ada3, pos3, gmix, w_b, w_gt, conv_w[l], conv_b[l].reshape(1, -1), b_col,
            g_q_lat[l].reshape(1, -1), g_kv_lat[l].reshape(1, -1), wuq, wk, wv, invf, sgn)
        hm = _mlstm(qm, km, vm, om, gr, g_mhead[l].reshape(1, -1))
        oa = _attn(qa, ka, va)
        h1 = _merge(h, ada3, hm, oa, gmix, w_g, w_branch_m[l].astype(BF16),
                    w_branch_a[l].astype(BF16), w_out[l].astype(BF16))
        last = l == depth - 1
        h = _ffn(h1, ada3, g_ffn[l].reshape(1, D), g_final.reshape(1, D),
                 w_ffn_in[l].astype(BF16), w_ffn_out[l].astype(BF16))
        assert last, "final rmsnorm is fused into the last layer's FFN kernel"
    return h
```

```python
import functools

import jax
import jax.numpy as jnp
import numpy as np
from jax import lax
from jax.experimental import pallas as pl
from jax.experimental.pallas import tpu as pltpu

F32 = jnp.float32
BF16 = jnp.bfloat16

D_MODEL = 1024
M_HEADS = 4
M_HEAD_DIM = 128
M_WIDTH = M_HEADS * M_HEAD_DIM
M_CONV = 4
A_HEADS = 8
A_NOPE = 64
A_ROPE = 32
A_V = 64
A_QK = A_NOPE + A_ROPE
A_Q_RANK = 256
A_KV_RANK = 128
A_WIDTH = A_HEADS * A_V
ROPE_THETA = 10000.0
D_FF = 2816
NORM_EPS = 1e-6

LANES = 128
HALO = 8
VMEM_LIMIT = 48 * 1024 * 1024

MIX_TM = 512
M_CHUNK = 128
ATT_T = 256
LOG2E = 1.4426950408889634
FFN_TM = 512
FFN_TF = 1408


def _sigmoid(x):
    return 1.0 / (1.0 + jnp.exp(-x))


def _silu(x):
    return x * _sigmoid(x)


def _log_sigmoid(x):
    return jnp.minimum(x, 0.0) - jnp.log1p(jnp.exp(-jnp.abs(x)))


def _rms(x, g):
    ms = jnp.mean(x * x, axis=-1, keepdims=True)
    return x * lax.rsqrt(ms + NORM_EPS) * g


def _bdot(a, b):
    return jnp.dot(a, b, preferred_element_type=F32)


def _dot_nt(a, b):
    return lax.dot_general(a, b, (((1,), (1,)), ((), ())), preferred_element_type=F32)


def _ada_kernel(c_ref, w_ref, b_ref, o_ref):
    c = c_ref[...]
    o_ref[...] = jnp.dot(_silu(c), w_ref[...], preferred_element_type=F32,
                         precision=lax.Precision.HIGHEST) + b_ref[...]


def _ada(c, w_ada, b_ada):
    B, D = c.shape
    N = w_ada.shape[1]
    return pl.pallas_call(
        _ada_kernel,
        grid=(N // D,),
        in_specs=[pl.BlockSpec((B, D), lambda j: (0, 0)),
                  pl.BlockSpec((D, D), lambda j: (0, j)),
                  pl.BlockSpec((1, D), lambda j: (0, j))],
        out_specs=pl.BlockSpec((B, D), lambda j: (0, j)),
        out_shape=jax.ShapeDtypeStruct((B, N), F32),
        compiler_params=pltpu.CompilerParams(dimension_semantics=("arbitrary",),
                                             vmem_limit_bytes=VMEM_LIMIT),
        name="ada",
    )(c, w_ada, b_ada.reshape(1, N))


def _rope_partner(blk, lane):
    up = pltpu.roll(blk, LANES - A_ROPE // 2, 1)
    dn = pltpu.roll(blk, A_ROPE // 2, 1)
    return jnp.where(lane < A_NOPE + A_ROPE // 2, up, dn)


def _mix_in_kernel(x_ref, xh_ref, ada_ref, pos_ref, gmix_ref, w_ref, wgt_ref, cw_ref, cb_ref,
                   bcol_ref, gq_ref, gkv_ref, wuq_ref, wk_ref, wv_ref, invf_ref, sgn_ref,
                   qm_ref, km_ref, vm_ref, om_ref, gr_ref, qa_ref, ka_ref, va_ref):
    i = pl.program_id(1)
    tm = x_ref.shape[1]
    sh1 = ada_ref[0, 0:1, :]
    sc1 = ada_ref[0, 1:2, :]
    g = gmix_ref[...]

    def modulate(xv):
        return _rms(xv, g) * (1.0 + sc1) + sh1

    u = modulate(x_ref[0]).astype(BF16)
    uh = modulate(xh_ref[0]).astype(BF16)

    wqk = w_ref[:, 0:2 * M_WIDTH]
    z = _bdot(u, wqk)
    zh = _bdot(uh, wqk)
    zh = jnp.where(i > 0, zh, 0.0)
    ext = jnp.concatenate([zh, z], axis=0)
    y = cb_ref[...]
    for j in range(M_CONV):
        sft = M_CONV - 1 - j
        y = y + ext[HALO - sft:HALO - sft + tm] * cw_ref[j:j + 1, :]
    qk = _silu(y)
    qm_ref[0] = (qk[:, :M_WIDTH] * (M_HEAD_DIM ** -0.5)).astype(BF16)
    km_ref[0] = qk[:, M_WIDTH:].astype(BF16)

    vm_ref[0] = _bdot(u, w_ref[:, 2 * M_WIDTH:3 * M_WIDTH]).astype(BF16)
    om_ref[0] = _bdot(u, w_ref[:, 3 * M_WIDTH:4 * M_WIDTH])

    gpre = _dot_nt(wgt_ref[...], u) + bcol_ref[...]
    row = lax.broadcasted_iota(jnp.int32, gpre.shape, 0)
    gr_ref[0] = jnp.where(row < M_HEADS, gpre, _log_sigmoid(gpre))

    lat = _bdot(u, w_ref[:, 4 * M_WIDTH:4 * M_WIDTH + 4 * LANES])
    q_lat = lat[:, 0:A_Q_RANK]
    kv_lat = lat[:, A_Q_RANK:A_Q_RANK + A_KV_RANK]
    misc = lat[:, A_Q_RANK + A_KV_RANK:]

    pos = pos_ref[0].astype(F32)
    ang = pos * invf_ref[...]
    cosv = jnp.cos(ang)
    sinv = jnp.sin(ang) * sgn_ref[...]
    lane = lax.broadcasted_iota(jnp.int32, (tm, LANES), 1)

    cq = _rms(q_lat, gq_ref[...]).astype(BF16)
    qa = _bdot(cq, wuq_ref[...])
    scale = A_QK ** -0.5 * LOG2E
    for h in range(A_HEADS):
        blk = qa[:, h * LANES:(h + 1) * LANES]
        rot = blk * cosv + _rope_partner(blk, lane) * sinv
        qa_ref[0, :, h * LANES:(h + 1) * LANES] = (rot * scale).astype(BF16)

    ckv = _rms(kv_lat, gkv_ref[...]).astype(BF16)
    kn = _bdot(ckv, wk_ref[...])
    va_ref[0] = _bdot(ckv, wv_ref[...]).astype(BF16)
    is_rope = (lane >= A_NOPE) & (lane < A_QK)
    krot = jnp.where(is_rope, misc * cosv + _rope_partner(misc, lane) * sinv, 0.0)
    for h in range(A_HEADS):
        ka_ref[0, :, h * LANES:(h + 1) * LANES] = (kn[:, h * LANES:(h + 1) * LANES] + krot).astype(BF16)


def _mix_in(x, ada3, pos3, g_mix, w_b, w_gt, conv_w, conv_b, b_col, g_q, g_kv, wuq, wk, wv,
            invf, sgn):
    B, S, D = x.shape
    tm = min(MIX_TM, S)
    nt = S // tm
    hb = tm // HALO
    const = lambda shape: pl.BlockSpec(shape, lambda b, i: (0,) * len(shape))
    tok = lambda w: pl.BlockSpec((1, tm, w), lambda b, i: (b, i, 0))
    out_shapes = (
        jax.ShapeDtypeStruct((B, S, M_WIDTH), BF16),
        jax.ShapeDtypeStruct((B, S, M_WIDTH), BF16),
        jax.ShapeDtypeStruct((B, S, M_WIDTH), BF16),
        jax.ShapeDtypeStruct((B, S, M_WIDTH), F32),
        jax.ShapeDtypeStruct((B, 2 * M_HEADS, S), F32),
        jax.ShapeDtypeStruct((B, S, A_HEADS * LANES), BF16),
        jax.ShapeDtypeStruct((B, S, A_HEADS * LANES), BF16),
        jax.ShapeDtypeStruct((B, S, A_WIDTH), BF16),
    )
    return pl.pallas_call(
        _mix_in_kernel,
        grid=(B, nt),
        in_specs=[tok(D),
                  pl.BlockSpec((1, HALO, D), lambda b, i: (b, jnp.maximum(i * hb - 1, 0), 0)),
                  pl.BlockSpec((1, 6, D), lambda b, i: (b, 0, 0)),
                  tok(1),
                  const((1, D)), const(w_b.shape), const(w_gt.shape), const(conv_w.shape),
                  const(conv_b.shape), const(b_col.shape), const(g_q.shape), const(g_kv.shape),
                  const(wuq.shape), const(wk.shape), const(wv.shape), const(invf.shape),
                  const(sgn.shape)],
        out_specs=(tok(M_WIDTH), tok(M_WIDTH), tok(M_WIDTH), tok(M_WIDTH),
                   pl.BlockSpec((1, 2 * M_HEADS, tm), lambda b, i: (b, 0, i)),
                   tok(A_HEADS * LANES), tok(A_HEADS * LANES), tok(A_WIDTH)),
        out_shape=out_shapes,
        compiler_params=pltpu.CompilerParams(dimension_semantics=("parallel", "arbitrary"),
                                             vmem_limit_bytes=VMEM_LIMIT),
        name="mix_in",
    )(x, x, ada3, pos3, g_mix, w_b, w_gt, conv_w, conv_b, b_col, g_q, g_kv, wuq, wk, wv, invf, sgn)


def _cumsum_lanes(x):
    lane = lax.broadcasted_iota(jnp.int32, x.shape, 1)
    sh = 1
    while sh < x.shape[1]:
        x = x + jnp.where(lane >= sh, pltpu.roll(x, sh, 1), 0.0)
        sh *= 2
    return x


def _mlstm_kernel(q_ref, k_ref, v_ref, om_ref, gr_ref, gmh_ref, o_ref, c_scr, m_scr):
    S = q_ref.shape[1]
    L = M_CHUNK
    dh = M_HEAD_DIM
    c_scr[...] = jnp.zeros_like(c_scr)
    m_scr[...] = jnp.zeros_like(m_scr)
    rows = lax.broadcasted_iota(jnp.int32, (L, L), 0)
    cols = lax.broadcasted_iota(jnp.int32, (L, L), 1)
    causal = rows >= cols
    one_col = (lax.broadcasted_iota(jnp.int32, (L, dh), 1) == 0).astype(BF16)
    grow = lax.broadcasted_iota(jnp.int32, (2 * M_HEADS, L), 0)

    def chunk(c, carry):
        off = pl.multiple_of(c * L, L)
        gr = gr_ref[0, :, pl.ds(off, L)]
        cs = _cumsum_lanes(gr)
        rowv = jnp.where(grow < M_HEADS, gr, cs)
        colv = jnp.concatenate([rowv] * (L // (2 * M_HEADS)), axis=0).T
        for h in range(M_HEADS):
            hs = slice(h * dh, (h + 1) * dh)
            q = q_ref[0, pl.ds(off, L), hs]
            k = k_ref[0, pl.ds(off, L), hs]
            v = v_ref[0, pl.ds(off, L), hs]
            li_row = rowv[h:h + 1, :]
            b_row = rowv[M_HEADS + h:M_HEADS + h + 1, :]
            li_col = colv[:, h:h + 1]
            b_col = colv[:, M_HEADS + h:M_HEADS + h + 1]
            m_prev = m_scr[h:h + 1, 0:1]
            c_aug = c_scr[h]

            dlog = jnp.where(causal, (b_col - b_row) + li_row, -jnp.inf)
            inter_log = b_col + m_prev
            m_t = jnp.maximum(inter_log, jnp.max(dlog, axis=-1, keepdims=True))
            dw = jnp.exp(dlog - m_t)
            inter_w = jnp.exp(inter_log - m_t)
            s = _dot_nt(q, k) * dw
            qc = _bdot(q, c_aug.astype(BF16))
            num = _bdot(s.astype(BF16), v) + inter_w * qc[:, :dh]
            den = jnp.sum(s, axis=-1, keepdims=True) + inter_w * qc[:, dh:dh + 1]
            hv = num / jnp.maximum(jnp.abs(den), jnp.exp(-m_t))
            hn = _rms(hv, gmh_ref[:, hs])
            o_ref[0, pl.ds(off, L), hs] = (_sigmoid(om_ref[0, pl.ds(off, L), hs]) * hn).astype(o_ref.dtype)

            b_end = b_row[:, L - 1:L]
            wlog_row = (b_end - b_row) + li_row
            m_new = jnp.maximum(b_end + m_prev, jnp.max(wlog_row, axis=-1, keepdims=True))
            w_col = jnp.exp((b_end - b_col) + li_col - m_new)
            decay = jnp.exp(b_end + m_prev - m_new)
            kw_t = (k.astype(F32) * w_col).T.astype(BF16)
            v_aug = jnp.concatenate([v, one_col], axis=1)
            c_scr[h] = decay * c_aug + _bdot(kw_t, v_aug)
            m_scr[h:h + 1, :] = jnp.broadcast_to(m_new, (1, LANES))
        return carry

    lax.fori_loop(0, S // L, chunk, 0)


def _mlstm(qm, km, vm, om, gr, g_mhead):
    B, S, W = qm.shape
    seq = lambda: pl.BlockSpec((1, S, W), lambda b: (b, 0, 0))
    return pl.pallas_call(
        _mlstm_kernel,
        grid=(B,),
        in_specs=[seq(), seq(), seq(), seq(),
                  pl.BlockSpec((1, 2 * M_HEADS, S), lambda b: (b, 0, 0)),
                  pl.BlockSpec((1, W), lambda b: (0, 0))],
        out_specs=seq(),
        out_shape=jax.ShapeDtypeStruct((B, S, W), BF16),
        scratch_shapes=[pltpu.VMEM((M_HEADS, M_HEAD_DIM, 2 * M_HEAD_DIM), F32),
                        pltpu.VMEM((2 * M_HEADS, LANES), F32)],
        compiler_params=pltpu.CompilerParams(dimension_semantics=("parallel",),
                                             vmem_limit_bytes=VMEM_LIMIT),
        name="mlstm",
    )(qm, km, vm, om, gr, g_mhead)


def _attn_kernel(q_ref, k_ref, v_ref, o_ref, m_scr, acc_scr):
    S = q_ref.shape[1]
    t = min(ATT_T, S)
    nq = S // t
    causal = (lax.broadcasted_iota(jnp.int32, (t, t), 1)
              <= lax.broadcasted_iota(jnp.int32, (t, t), 0))

    def tile(i):
        return pl.ds(pl.multiple_of(i * t, t), t)

    def head(hh):
        return slice(hh * LANES, (hh + 1) * LANES)

    def scores(qi, kt):
        return tuple(_dot_nt(q_ref[0, tile(qi), head(hh)], k_ref[0, tile(kt), head(hh)])
                     for hh in range(2))

    def softmax(qi, s, diag):
        ps, alphas = [], []
        for hh in range(2):
            sh = s[hh]
            if diag:
                sh = jnp.where(causal, sh, -jnp.inf)
                row_max = jnp.max(sh, axis=-1, keepdims=True)
                m_new = jnp.broadcast_to(row_max, (t, LANES))
                p = jnp.exp2(sh - row_max)
            else:
                m_old = m_scr[hh, tile(qi), :]
                m_new = jnp.maximum(m_old, jnp.max(sh, axis=-1, keepdims=True))
                alpha = jnp.exp2(m_old - m_new)
                alphas.append(jnp.concatenate([alpha, alpha], axis=1))
                p = jnp.exp2(sh - jnp.concatenate([m_new] * (t // LANES), axis=1))
            m_scr[hh, tile(qi), :] = m_new
            ps.append(p.astype(BF16))
        return jnp.concatenate(ps, axis=0), tuple(alphas)

    ones = jnp.ones((t, LANES), BF16)

    def values(qi, kt, p, alphas, diag):
        pv = _bdot(p, jnp.concatenate([v_ref[0, tile(kt), :], ones], axis=1))
        for hh in range(2):
            part = pv[hh * t:(hh + 1) * t]
            if diag:
                acc_scr[hh, tile(qi), :] = part
            else:
                acc_scr[hh, tile(qi), :] = alphas[hh] * acc_scr[hh, tile(qi), :] + part

    def run(n, first, advance, diag):
        if n == 0:
            return
        b0 = first
        s = scores(*b0)
        p, al = softmax(b0[0], s, diag)
        if n == 1:
            values(*b0, p, al, diag)
            return
        b1 = advance(*b0)
        s = scores(*b1)

        def body(_, carry):
            bc, bb, s, p, al = carry
            values(*bc, p, al, diag)
            p, al = softmax(bb[0], s, diag)
            ba = advance(*bb)
            return bb, ba, scores(*ba), p, al

        bc, bb, s, p, al = lax.fori_loop(2, n, body, (b0, b1, s, p, al))
        values(*bc, p, al, diag)
        p, al = softmax(bb[0], s, diag)
        values(*bb, p, al, diag)

    def next_lower(qi, kt):
        wrap = kt + 1 >= qi
        return jnp.where(wrap, qi + 1, qi), jnp.where(wrap, 0, kt + 1)

    zero, one = jnp.int32(0), jnp.int32(1)
    run(nq, (zero, zero), lambda qi, kt: (qi + 1, kt + 1), True)
    run(nq * (nq - 1) // 2, (one, zero), next_lower, False)

    lane = lax.broadcasted_iota(jnp.int32, (S, LANES), 1)
    outs = [acc_scr[hh, :, :LANES] / acc_scr[hh, :, LANES:] for hh in range(2)]
    o_ref[0] = jnp.where(lane < A_V, outs[0], outs[1]).astype(o_ref.dtype)


def _attn(qa, ka, va):
    B, S, _ = qa.shape
    pair = lambda w: pl.BlockSpec((1, S, w), lambda b, p: (b, 0, p))
    return pl.pallas_call(
        _attn_kernel,
        grid=(B, A_HEADS // 2),
        in_specs=[pair(2 * LANES), pair(2 * LANES), pair(LANES)],
        out_specs=pair(LANES),
        out_shape=jax.ShapeDtypeStruct((B, S, A_WIDTH), BF16),
        scratch_shapes=[pltpu.VMEM((2, S, LANES), F32), pltpu.VMEM((2, S, 2 * LANES), F32)],
        compiler_params=pltpu.CompilerParams(dimension_semantics=("parallel", "parallel"),
                                             vmem_limit_bytes=VMEM_LIMIT),
        name="attn",
    )(qa, ka, va)


def _merge_kernel(x_ref, ada_ref, hm_ref, oa_ref, gmix_ref, wg_ref, wbm_ref, wba_ref, wo_ref, h_ref):
    x = x_ref[0]
    sh1 = ada_ref[0, 0:1, :]
    sc1 = ada_ref[0, 1:2, :]
    gt1 = ada_ref[0, 2:3, :]
    u = (_rms(x, gmix_ref[...]) * (1.0 + sc1) + sh1).astype(BF16)
    D = x.shape[1]
    y = _sigmoid(_bdot(u, wg_ref[:, 0:D])) * _bdot(hm_ref[0], wbm_ref[...])
    y = y + _sigmoid(_bdot(u, wg_ref[:, D:2 * D])) * _bdot(oa_ref[0], wba_ref[...])
    h_ref[0] = x + gt1 * _bdot(y.astype(BF16), wo_ref[...])


def _merge(x, ada3, hm, oa, g_mix, w_g, w_bm, w_ba, w_o):
    B, S, D = x.shape
    tm = min(MIX_TM, S)
    const = lambda shape: pl.BlockSpec(shape, lambda b, i: (0,) * len(shape))
    tok = lambda w: pl.BlockSpec((1, tm, w), lambda b, i: (b, i, 0))
    return pl.pallas_call(
        _merge_kernel,
        grid=(B, S // tm),
        in_specs=[tok(D), pl.BlockSpec((1, 6, D), lambda b, i: (b, 0, 0)), tok(M_WIDTH), tok(A_WIDTH),
                  const((1, D)), const(w_g.shape), const(w_bm.shape), const(w_ba.shape),
                  const(w_o.shape)],
        out_specs=tok(D),
        out_shape=jax.ShapeDtypeStruct((B, S, D), F32),
        compiler_params=pltpu.CompilerParams(dimension_semantics=("parallel", "parallel"),
                                             vmem_limit_bytes=VMEM_LIMIT),
        name="merge",
    )(x, ada3, hm, oa, g_mix, w_g, w_bm, w_ba, w_o)


def _ffn_kernel(h_ref, ada_ref, gffn_ref, gfin_ref, wg_ref, wu_ref, wo_ref, o_ref, u_scr, acc_scr):
    j = pl.program_id(2)

    @pl.when(j == 0)
    def _():
        sh2 = ada_ref[0, 3:4, :]
        sc2 = ada_ref[0, 4:5, :]
        u_scr[...] = (_rms(h_ref[0], gffn_ref[...]) * (1.0 + sc2) + sh2).astype(BF16)
        acc_scr[...] = jnp.zeros_like(acc_scr)

    u = u_scr[...]
    a = _silu(_bdot(u, wg_ref[...])) * _bdot(u, wu_ref[...])
    acc_scr[...] += _bdot(a.astype(BF16), wo_ref[...])

    @pl.when(j == pl.num_programs(2) - 1)
    def _():
        gt2 = ada_ref[0, 5:6, :]
        h2 = h_ref[0] + gt2 * acc_scr[...]
        o_ref[0] = _rms(h2, gfin_ref[...])


def _ffn(h1, ada3, g_ffn, g_final, w_in, w_out):
    B, S, D = h1.shape
    tm = min(FFN_TM, S)
    nf = D_FF // FFN_TF
    tok = pl.BlockSpec((1, tm, D), lambda b, i, j: (b, i, 0))
    return pl.pallas_call(
        _ffn_kernel,
        grid=(B, S // tm, nf),
        in_specs=[tok,
                  pl.BlockSpec((1, 6, D), lambda b, i, j: (b, 0, 0)),
                  pl.BlockSpec((1, D), lambda b, i, j: (0, 0)),
                  pl.BlockSpec((1, D), lambda b, i, j: (0, 0)),
                  pl.BlockSpec((D, FFN_TF), lambda b, i, j: (0, j)),
                  pl.BlockSpec((D, FFN_TF), lambda b, i, j: (0, j + nf)),
                  pl.BlockSpec((FFN_TF, D), lambda b, i, j: (j, 0))],
        out_specs=tok,
        out_shape=jax.ShapeDtypeStruct((B, S, D), F32),
        scratch_shapes=[pltpu.VMEM((tm, D), BF16), pltpu.VMEM((tm, D), F32)],
        compiler_params=pltpu.CompilerParams(
            dimension_semantics=("parallel", "parallel", "arbitrary"),
            vmem_limit_bytes=VMEM_LIMIT),
        name="ffn",
    )(h1, ada3, g_ffn, g_final, w_in, w_in, w_out)


def _pack_weights(w_in, w_uq, w_ukv):
    D = w_in.shape[0]
    o_gate = 4 * M_WIDTH
    o_qlat = o_gate + 2 * M_HEADS
    o_kr = o_qlat + A_Q_RANK + A_KV_RANK
    o_gm = o_kr + A_ROPE
    zeros = lambda n: jnp.zeros((D, n), w_in.dtype)
    misc = jnp.concatenate([zeros(A_NOPE), w_in[:, o_kr:o_gm], zeros(LANES - A_QK)], axis=1)
    w_b = jnp.concatenate([w_in[:, :o_gate], w_in[:, o_qlat:o_kr], misc], axis=1).astype(BF16)
    w_gt = w_in[:, o_gate:o_qlat].T.astype(BF16)
    w_g = w_in[:, o_gm:].astype(BF16)
    wuq = jnp.pad(w_uq.reshape(A_Q_RANK, A_HEADS, A_QK), ((0, 0), (0, 0), (0, LANES - A_QK)))
    wuq = wuq.reshape(A_Q_RANK, A_HEADS * LANES).astype(BF16)
    wkv = w_ukv.reshape(A_KV_RANK, A_HEADS, A_NOPE + A_V)
    wk = jnp.pad(wkv[:, :, :A_NOPE], ((0, 0), (0, 0), (0, LANES - A_NOPE)))
    wk = wk.reshape(A_KV_RANK, A_HEADS * LANES).astype(BF16)
    wv = wkv[:, :, A_NOPE:].reshape(A_KV_RANK, A_WIDTH).astype(BF16)
    return w_b, w_gt, w_g, wuq, wk, wv


def _rope_rows():
    inv_freq = ROPE_THETA ** (-np.arange(0, A_ROPE, 2, dtype=np.float32) / A_ROPE)
    half = A_ROPE // 2
    invf = np.zeros((1, LANES), np.float32)
    invf[0, A_NOPE:A_NOPE + half] = inv_freq
    invf[0, A_NOPE + half:A_QK] = inv_freq
    sgn = np.zeros((1, LANES), np.float32)
    sgn[0, A_NOPE:A_NOPE + half] = -1.0
    sgn[0, A_NOPE + half:A_QK] = 1.0
    return invf, sgn


def kernel(x, c, positions, w_ada, b_ada, g_mix, w_in, conv_w, conv_b, b_igate, b_fgate, g_mhead,
           g_q_lat, w_uq, g_kv_lat, w_ukv, w_branch_m, w_branch_a, w_out, g_ffn, w_ffn_in,
           w_ffn_out, g_final):
    B, S, D = x.shape
    depth = w_ada.shape[0]
    inv_freq = ROPE_THETA ** (-jnp.arange(0, A_ROPE, 2, dtype=F32) / A_ROPE)
    zpad = lambda n: jnp.zeros((n,), F32)
    invf = jnp.concatenate([zpad(A_NOPE), inv_freq, inv_freq, zpad(LANES - A_QK)]).reshape(1, LANES)
    half = A_ROPE // 2
    sgn = jnp.concatenate([zpad(A_NOPE), -jnp.ones((half,), F32), jnp.ones((half,), F32),
                           zpad(LANES - A_QK)]).reshape(1, LANES)
    pos3 = positions.reshape(B, S, 1)
    h = x
    for l in range(depth):
        ada3 = _ada(c, w_ada[l], b_ada[l]).reshape(B, 6, D)
        w_b, w_gt, w_g, wuq, wk, wv = _pack_weights(w_in[l], w_uq[l], w_ukv[l])
        b_col = jnp.concatenate([b_igate[l], b_fgate[l]]).astype(F32).reshape(2 * M_HEADS, 1)
        gmix = g_mix[l].reshape(1, D)
        qm, km, vm, om, gr, qa, ka, va = _mix_in(
            h, ada3, pos3, gmix, w_b, w_gt, conv_w[l], conv_b[l].reshape(1, -1), b_col,
            g_q_lat[l].reshape(1, -1), g_kv_lat[l].reshape(1, -1), wuq, wk, wv, invf, sgn)
        hm = _mlstm(qm, km, vm, om, gr, g_mhead[l].reshape(1, -1))
        oa = _attn(qa, ka, va)
        h1 = _merge(h, ada3, hm, oa, gmix, w_g, w_branch_m[l].astype(BF16),
                    w_branch_a[l].astype(BF16), w_out[l].astype(BF16))
        last = l == depth - 1
        h = _ffn(h1, ada3, g_ffn[l].reshape(1, D), g_final.reshape(1, D),
                 w_ffn_in[l].astype(BF16), w_ffn_out[l].astype(BF16))
        assert last, "final rmsnorm is fused into the last layer's FFN kernel"
    return h
```

```python
import functools

import jax
import jax.numpy as jnp
import numpy as np
from jax import lax
from jax.experimental import pallas as pl
from jax.experimental.pallas import tpu as pltpu

F32 = jnp.float32
BF16 = jnp.bfloat16

D_MODEL = 1024
M_HEADS = 4
M_HEAD_DIM = 128
M_WIDTH = M_HEADS * M_HEAD_DIM
M_CONV = 4
A_HEADS = 8
A_NOPE = 64
A_ROPE = 32
A_V = 64
A_QK = A_NOPE + A_ROPE
A_Q_RANK = 256
A_KV_RANK = 128
A_WIDTH = A_HEADS * A_V
ROPE_THETA = 10000.0
D_FF = 2816
NORM_EPS = 1e-6

LANES = 128
HALO = 8
VMEM_LIMIT = 56 * 1024 * 1024

MIX_TM = 512
M_CHUNK = 128
ATT_T = 256
LOG2E = 1.4426950408889634
FFN_TM = 512
FFN_TF = 1408


def _sigmoid(x):
    return 1.0 / (1.0 + jnp.exp(-x))


def _silu(x):
    return x * _sigmoid(x)


def _log_sigmoid(x):
    return jnp.minimum(x, 0.0) - jnp.log1p(jnp.exp(-jnp.abs(x)))


def _rms(x, g):
    ms = jnp.mean(x * x, axis=-1, keepdims=True)
    return x * lax.rsqrt(ms + NORM_EPS) * g


def _bdot(a, b):
    return jnp.dot(a, b, preferred_element_type=F32)


def _dot_nt(a, b):
    return lax.dot_general(a, b, (((1,), (1,)), ((), ())), preferred_element_type=F32)


def _ada_kernel(c_ref, w_ref, b_ref, o_ref):
    c = c_ref[...]
    o_ref[...] = jnp.dot(_silu(c), w_ref[...], preferred_element_type=F32,
                         precision=lax.Precision.HIGHEST) + b_ref[...]


def _ada(c, w_ada, b_ada):
    B, D = c.shape
    N = w_ada.shape[1]
    return pl.pallas_call(
        _ada_kernel,
        grid=(N // D,),
        in_specs=[pl.BlockSpec((B, D), lambda j: (0, 0)),
                  pl.BlockSpec((D, D), lambda j: (0, j)),
                  pl.BlockSpec((1, D), lambda j: (0, j))],
        out_specs=pl.BlockSpec((B, D), lambda j: (0, j)),
        out_shape=jax.ShapeDtypeStruct((B, N), F32),
        compiler_params=pltpu.CompilerParams(dimension_semantics=("arbitrary",),
                                             vmem_limit_bytes=VMEM_LIMIT),
        name="ada",
    )(c, w_ada, b_ada.reshape(1, N))


def _rope_partner(blk, lane):
    up = pltpu.roll(blk, LANES - A_ROPE // 2, 1)
    dn = pltpu.roll(blk, A_ROPE // 2, 1)
    return jnp.where(lane < A_NOPE + A_ROPE // 2, up, dn)


def _mix_in_kernel(x_ref, xh_ref, ada_ref, pos_ref, gmix_ref, w_ref, wgt_ref, cw_ref, cb_ref,
                   bcol_ref, gq_ref, gkv_ref, wuq_ref, wk_ref, wv_ref, invf_ref, sgn_ref,
                   qm_ref, km_ref, vm_ref, om_ref, gr_ref, qa_ref, ka_ref, va_ref):
    i = pl.program_id(1)
    tm = x_ref.shape[1]
    sh1 = ada_ref[0, 0:1, :]
    sc1 = ada_ref[0, 1:2, :]
    g = gmix_ref[...]

    def modulate(xv):
        return _rms(xv, g) * (1.0 + sc1) + sh1

    u = modulate(x_ref[0]).astype(BF16)
    uh = modulate(xh_ref[0]).astype(BF16)

    wqk = w_ref[:, 0:2 * M_WIDTH]
    z = _bdot(u, wqk)
    zh = _bdot(uh, wqk)
    zh = jnp.where(i > 0, zh, 0.0)
    ext = jnp.concatenate([zh, z], axis=0)
    y = cb_ref[...]
    for j in range(M_CONV):
        sft = M_CONV - 1 - j
        y = y + ext[HALO - sft:HALO - sft + tm] * cw_ref[j:j + 1, :]
    qk = _silu(y)
    qm_ref[0] = (qk[:, :M_WIDTH] * (M_HEAD_DIM ** -0.5)).astype(BF16)
    km_ref[0] = qk[:, M_WIDTH:].astype(BF16)

    vm_ref[0] = _bdot(u, w_ref[:, 2 * M_WIDTH:3 * M_WIDTH]).astype(BF16)
    om_ref[0] = _bdot(u, w_ref[:, 3 * M_WIDTH:4 * M_WIDTH])

    gpre = _dot_nt(wgt_ref[...], u) + bcol_ref[...]
    row = lax.broadcasted_iota(jnp.int32, gpre.shape, 0)
    gr_ref[0] = jnp.where(row < M_HEADS, gpre, _log_sigmoid(gpre))

    lat = _bdot(u, w_ref[:, 4 * M_WIDTH:4 * M_WIDTH + 4 * LANES])
    q_lat = lat[:, 0:A_Q_RANK]
    kv_lat = lat[:, A_Q_RANK:A_Q_RANK + A_KV_RANK]
    misc = lat[:, A_Q_RANK + A_KV_RANK:]

    pos = pos_ref[0].astype(F32)
    ang = pos * invf_ref[...]
    cosv = jnp.cos(ang)
    sinv = jnp.sin(ang) * sgn_ref[...]
    lane = lax.broadcasted_iota(jnp.int32, (tm, LANES), 1)

    cq = _rms(q_lat, gq_ref[...]).astype(BF16)
    qa = _bdot(cq, wuq_ref[...])
    scale = A_QK ** -0.5 * LOG2E
    for h in range(A_HEADS):
        blk = qa[:, h * LANES:(h + 1) * LANES]
        rot = blk * cosv + _rope_partner(blk, lane) * sinv
        qa_ref[0, :, h * LANES:(h + 1) * LANES] = (rot * scale).astype(BF16)

    ckv = _rms(kv_lat, gkv_ref[...]).astype(BF16)
    kn = _bdot(ckv, wk_ref[...])
    va_ref[0] = _bdot(ckv, wv_ref[...]).astype(BF16)
    is_rope = (lane >= A_NOPE) & (lane < A_QK)
    krot = jnp.where(is_rope, misc * cosv + _rope_partner(misc, lane) * sinv, 0.0)
    for h in range(A_HEADS):
        ka_ref[0, :, h * LANES:(h + 1) * LANES] = (kn[:, h * LANES:(h + 1) * LANES] + krot).astype(BF16)


def _mix_in(x, ada3, pos3, g_mix, w_b, w_gt, conv_w, conv_b, b_col, g_q, g_kv, wuq, wk, wv,
            invf, sgn):
    B, S, D = x.shape
    tm = min(MIX_TM, S)
    nt = S // tm
    hb = tm // HALO
    const = lambda shape: pl.BlockSpec(shape, lambda b, i: (0,) * len(shape))
    tok = lambda w: pl.BlockSpec((1, tm, w), lambda b, i: (b, i, 0))
    out_shapes = (
        jax.ShapeDtypeStruct((B, S, M_WIDTH), BF16),
        jax.ShapeDtypeStruct((B, S, M_WIDTH), BF16),
        jax.ShapeDtypeStruct((B, S, M_WIDTH), BF16),
        jax.ShapeDtypeStruct((B, S, M_WIDTH), F32),
        jax.ShapeDtypeStruct((B, 2 * M_HEADS, S), F32),
        jax.ShapeDtypeStruct((B, S, A_HEADS * LANES), BF16),
        jax.ShapeDtypeStruct((B, S, A_HEADS * LANES), BF16),
        jax.ShapeDtypeStruct((B, S, A_WIDTH), BF16),
    )
    return pl.pallas_call(
        _mix_in_kernel,
        grid=(B, nt),
        in_specs=[tok(D),
                  pl.BlockSpec((1, HALO, D), lambda b, i: (b, jnp.maximum(i * hb - 1, 0), 0)),
                  pl.BlockSpec((1, 6, D), lambda b, i: (b, 0, 0)),
                  tok(1),
                  const((1, D)), const(w_b.shape), const(w_gt.shape), const(conv_w.shape),
                  const(conv_b.shape), const(b_col.shape), const(g_q.shape), const(g_kv.shape),
                  const(wuq.shape), const(wk.shape), const(wv.shape), const(invf.shape),
                  const(sgn.shape)],
        out_specs=(tok(M_WIDTH), tok(M_WIDTH), tok(M_WIDTH), tok(M_WIDTH),
                   pl.BlockSpec((1, 2 * M_HEADS, tm), lambda b, i: (b, 0, i)),
                   tok(A_HEADS * LANES), tok(A_HEADS * LANES), tok(A_WIDTH)),
        out_shape=out_shapes,
        compiler_params=pltpu.CompilerParams(dimension_semantics=("parallel", "arbitrary"),
                                             vmem_limit_bytes=VMEM_LIMIT),
        name="mix_in",
    )(x, x, ada3, pos3, g_mix, w_b, w_gt, conv_w, conv_b, b_col, g_q, g_kv, wuq, wk, wv, invf, sgn)


def _cumsum_lanes(x):
    lane = lax.broadcasted_iota(jnp.int32, x.shape, 1)
    sh = 1
    while sh < x.shape[1]:
        x = x + jnp.where(lane >= sh, pltpu.roll(x, sh, 1), 0.0)
        sh *= 2
    return x


def _mlstm_kernel(q_ref, k_ref, v_ref, om_ref, gr_ref, gmh_ref, o_ref, c_scr, m_scr):
    S = q_ref.shape[1]
    L = M_CHUNK
    dh = M_HEAD_DIM
    c_scr[...] = jnp.zeros_like(c_scr)
    m_scr[...] = jnp.zeros_like(m_scr)
    rows = lax.broadcasted_iota(jnp.int32, (L, L), 0)
    cols = lax.broadcasted_iota(jnp.int32, (L, L), 1)
    causal = rows >= cols
    one_col = (lax.broadcasted_iota(jnp.int32, (L, dh), 1) == 0).astype(BF16)
    grow = lax.broadcasted_iota(jnp.int32, (2 * M_HEADS, L), 0)

    def chunk(c, carry):
        off = pl.multiple_of(c * L, L)
        gr = gr_ref[0, :, pl.ds(off, L)]
        cs = _cumsum_lanes(gr)
        rowv = jnp.where(grow < M_HEADS, gr, cs)
        colv = jnp.concatenate([rowv] * (L // (2 * M_HEADS)), axis=0).T
        for h in range(M_HEADS):
            hs = slice(h * dh, (h + 1) * dh)
            q = q_ref[0, pl.ds(off, L), hs]
            k = k_ref[0, pl.ds(off, L), hs]
            v = v_ref[0, pl.ds(off, L), hs]
            li_row = rowv[h:h + 1, :]
            b_row = rowv[M_HEADS + h:M_HEADS + h + 1, :]
            li_col = colv[:, h:h + 1]
            b_col = colv[:, M_HEADS + h:M_HEADS + h + 1]
            m_prev = m_scr[h:h + 1, 0:1]
            c_aug = c_scr[h]

            dlog = jnp.where(causal, (b_col - b_row) + li_row, -jnp.inf)
            inter_log = b_col + m_prev
            m_t = jnp.maximum(inter_log, jnp.max(dlog, axis=-1, keepdims=True))
            dw = jnp.exp(dlog - m_t)
            inter_w = jnp.exp(inter_log - m_t)
            s = _dot_nt(q, k) * dw
            qc = _bdot(q, c_aug.astype(BF16))
            num = _bdot(s.astype(BF16), v) + inter_w * qc[:, :dh]
            den = jnp.sum(s, axis=-1, keepdims=True) + inter_w * qc[:, dh:dh + 1]
            hv = num / jnp.maximum(jnp.abs(den), jnp.exp(-m_t))
            hn = _rms(hv, gmh_ref[:, hs])
            o_ref[0, pl.ds(off, L), hs] = (_sigmoid(om_ref[0, pl.ds(off, L), hs]) * hn).astype(o_ref.dtype)

            b_end = b_row[:, L - 1:L]
            wlog_row = (b_end - b_row) + li_row
            m_new = jnp.maximum(b_end + m_prev, jnp.max(wlog_row, axis=-1, keepdims=True))
            w_col = jnp.exp((b_end - b_col) + li_col - m_new)
            decay = jnp.exp(b_end + m_prev - m_new)
            kw_t = (k.astype(F32) * w_col).T.astype(BF16)
            v_aug = jnp.concatenate([v, one_col], axis=1)
            c_scr[h] = decay * c_aug + _bdot(kw_t, v_aug)
            m_scr[h:h + 1, :] = jnp.broadcast_to(m_new, (1, LANES))
        return carry

    lax.fori_loop(0, S // L, chunk, 0)


def _mlstm(qm, km, vm, om, gr, g_mhead):
    B, S, W = qm.shape
    seq = lambda: pl.BlockSpec((1, S, W), lambda b: (b, 0, 0))
    return pl.pallas_call(
        _mlstm_kernel,
        grid=(B,),
        in_specs=[seq(), seq(), seq(), seq(),
                  pl.BlockSpec((1, 2 * M_HEADS, S), lambda b: (b, 0, 0)),
                  pl.BlockSpec((1, W), lambda b: (0, 0))],
        out_specs=seq(),
        out_shape=jax.ShapeDtypeStruct((B, S, W), BF16),
        scratch_shapes=[pltpu.VMEM((M_HEADS, M_HEAD_DIM, 2 * M_HEAD_DIM), F32),
                        pltpu.VMEM((2 * M_HEADS, LANES), F32)],
        compiler_params=pltpu.CompilerParams(dimension_semantics=("parallel",),
                                             vmem_limit_bytes=VMEM_LIMIT),
        name="mlstm",
    )(qm, km, vm, om, gr, g_mhead)


def _attn_kernel(q_ref, k_ref, v_ref, o_ref, m_scr, acc_scr, s_scr, p_scr, al_scr):
    S = q_ref.shape[1]
    t = min(ATT_T, S)
    nq = S // t
    causal = (lax.broadcasted_iota(jnp.int32, (t, t), 1)
              <= lax.broadcasted_iota(jnp.int32, (t, t), 0))

    def tile(i):
        return pl.ds(pl.multiple_of(i * t, t), t)

    def head(hh):
        return slice(hh * LANES, (hh + 1) * LANES)

    def scores(blk, slot):
        qi, kt = blk
        for hh in range(2):
            s_scr[slot, hh] = _dot_nt(q_ref[0, tile(qi), head(hh)], k_ref[0, tile(kt), head(hh)])

    def softmax(blk, slot, diag):
        qi, _ = blk
        for hh in range(2):
            mask = (lambda x: jnp.where(causal, x, -jnp.inf)) if diag else (lambda x: x)
            row_max = jnp.max(mask(s_scr[slot, hh]), axis=-1, keepdims=True)
            if diag:
                m_new = jnp.broadcast_to(row_max, (t, LANES))
            else:
                m_old = m_scr[hh, tile(qi), :]
                m_new = jnp.maximum(m_old, row_max)
                al_scr[slot, hh] = jnp.exp2(m_old - m_new)
            m_scr[hh, tile(qi), :] = m_new
            p = jnp.exp2(mask(s_scr[slot, hh]) - jnp.concatenate([m_new] * (t // LANES), axis=1))
            p_scr[slot, hh * t:(hh + 1) * t, :] = p.astype(BF16)

    ones = jnp.ones((t, LANES), BF16)

    def values(blk, slot, diag):
        qi, kt = blk
        pv = _bdot(p_scr[slot], jnp.concatenate([v_ref[0, tile(kt), :], ones], axis=1))
        for hh in range(2):
            part = pv[hh * t:(hh + 1) * t]
            if diag:
                acc_scr[hh, tile(qi), :] = part
            else:
                alpha = al_scr[slot, hh]
                acc_scr[hh, tile(qi), :] = (jnp.concatenate([alpha, alpha], axis=1)
                                            * acc_scr[hh, tile(qi), :] + part)

    def run(n, first, advance, diag):
        if n == 0:
            return

        def step(bc, bb, ba, slot):
            values(bc, slot, diag)
            softmax(bb, 1 - slot, diag)
            scores(ba, slot)

        b0 = first
        scores(b0, 0)
        softmax(b0, 0, diag)
        if n == 1:
            values(b0, 0, diag)
            return
        b1 = advance(*b0)
        scores(b1, 1)

        def body(_, carry):
            bc, bb = carry
            ba = advance(*bb)
            step(bc, bb, ba, 0)
            bz = advance(*ba)
            step(bb, ba, bz, 1)
            return ba, bz

        bc, bb = lax.fori_loop(0, (n - 2) // 2, body, (b0, b1))
        if (n - 2) % 2:
            ba = advance(*bb)
            step(bc, bb, ba, 0)
            bc, bb = bb, ba
        last = (n - 1) % 2
        values(bc, 1 - last, diag)
        softmax(bb, last, diag)
        values(bb, last, diag)

    def next_lower(qi, kt):
        wrap = kt + 1 >= qi
        return jnp.where(wrap, qi + 1, qi), jnp.where(wrap, 0, kt + 1)

    zero, one = jnp.int32(0), jnp.int32(1)
    run(nq, (zero, zero), lambda qi, kt: (qi + 1, kt + 1), True)
    run(nq * (nq - 1) // 2, (one, zero), next_lower, False)

    lane = lax.broadcasted_iota(jnp.int32, (S, LANES), 1)
    outs = [acc_scr[hh, :, :LANES] / acc_scr[hh, :, LANES:] for hh in range(2)]
    o_ref[0] = jnp.where(lane < A_V, outs[0], outs[1]).astype(o_ref.dtype)


def _attn(qa, ka, va):
    B, S, _ = qa.shape
    t = min(ATT_T, S)
    pair = lambda w: pl.BlockSpec((1, S, w), lambda b, p: (b, 0, p))
    return pl.pallas_call(
        _attn_kernel,
        grid=(B, A_HEADS // 2),
        in_specs=[pair(2 * LANES), pair(2 * LANES), pair(LANES)],
        out_specs=pair(LANES),
        out_shape=jax.ShapeDtypeStruct((B, S, A_WIDTH), BF16),
        scratch_shapes=[pltpu.VMEM((2, S, LANES), F32), pltpu.VMEM((2, S, 2 * LANES), F32),
                        pltpu.VMEM((2, 2, t, t), F32), pltpu.VMEM((2, 2 * t, t), BF16),
                        pltpu.VMEM((2, 2, t, LANES), F32)],
        compiler_params=pltpu.CompilerParams(dimension_semantics=("parallel", "parallel"),
                                             vmem_limit_bytes=VMEM_LIMIT),
        name="attn",
    )(qa, ka, va)


def _merge_kernel(x_ref, ada_ref, hm_ref, oa_ref, gmix_ref, wg_ref, wbm_ref, wba_ref, wo_ref, h_ref):
    x = x_ref[0]
    sh1 = ada_ref[0, 0:1, :]
    sc1 = ada_ref[0, 1:2, :]
    gt1 = ada_ref[0, 2:3, :]
    u = (_rms(x, gmix_ref[...]) * (1.0 + sc1) + sh1).astype(BF16)
    D = x.shape[1]
    y = _sigmoid(_bdot(u, wg_ref[:, 0:D])) * _bdot(hm_ref[0], wbm_ref[...])
    y = y + _sigmoid(_bdot(u, wg_ref[:, D:2 * D])) * _bdot(oa_ref[0], wba_ref[...])
    h_ref[0] = x + gt1 * _bdot(y.astype(BF16), wo_ref[...])


def _merge(x, ada3, hm, oa, g_mix, w_g, w_bm, w_ba, w_o):
    B, S, D = x.shape
    tm = min(MIX_TM, S)
    const = lambda shape: pl.BlockSpec(shape, lambda b, i: (0,) * len(shape))
    tok = lambda w: pl.BlockSpec((1, tm, w), lambda b, i: (b, i, 0))
    return pl.pallas_call(
        _merge_kernel,
        grid=(B, S // tm),
        in_specs=[tok(D), pl.BlockSpec((1, 6, D), lambda b, i: (b, 0, 0)), tok(M_WIDTH), tok(A_WIDTH),
                  const((1, D)), const(w_g.shape), const(w_bm.shape), const(w_ba.shape),
                  const(w_o.shape)],
        out_specs=tok(D),
        out_shape=jax.ShapeDtypeStruct((B, S, D), F32),
        compiler_params=pltpu.CompilerParams(dimension_semantics=("parallel", "parallel"),
                                             vmem_limit_bytes=VMEM_LIMIT),
        name="merge",
    )(x, ada3, hm, oa, g_mix, w_g, w_bm, w_ba, w_o)


def _ffn_kernel(h_ref, ada_ref, gffn_ref, gfin_ref, wi_ref, wo_ref, o_ref):
    h = h_ref[0]
    sh2 = ada_ref[0, 3:4, :]
    sc2 = ada_ref[0, 4:5, :]
    gt2 = ada_ref[0, 5:6, :]
    u = (_rms(h, gffn_ref[...]) * (1.0 + sc2) + sh2).astype(BF16)
    acc = None
    for j in range(D_FF // FFN_TF):
        lo = j * FFN_TF
        a = _silu(_bdot(u, wi_ref[:, lo:lo + FFN_TF])) * _bdot(u, wi_ref[:, D_FF + lo:D_FF + lo + FFN_TF])
        part = _bdot(a.astype(BF16), wo_ref[lo:lo + FFN_TF, :])
        acc = part if acc is None else acc + part
    o_ref[0] = _rms(h + gt2 * acc, gfin_ref[...])


def _ffn(h1, ada3, g_ffn, g_final, w_in, w_out):
    B, S, D = h1.shape
    tm = min(FFN_TM, S)
    tok = pl.BlockSpec((1, tm, D), lambda b, i: (b, i, 0))
    resident = lambda shape: pl.BlockSpec(shape, lambda b, i: (0, 0), pipeline_mode=pl.Buffered(1))
    return pl.pallas_call(
        _ffn_kernel,
        grid=(B, S // tm),
        in_specs=[tok,
                  pl.BlockSpec((1, 6, D), lambda b, i: (b, 0, 0)),
                  resident((1, D)), resident((1, D)), resident(w_in.shape), resident(w_out.shape)],
        out_specs=tok,
        out_shape=jax.ShapeDtypeStruct((B, S, D), F32),
        compiler_params=pltpu.CompilerParams(dimension_semantics=("parallel", "parallel"),
                                             vmem_limit_bytes=VMEM_LIMIT),
        name="ffn",
    )(h1, ada3, g_ffn, g_final, w_in, w_out)


def _pack_weights(w_in, w_uq, w_ukv):
    D = w_in.shape[0]
    o_gate = 4 * M_WIDTH
    o_qlat = o_gate + 2 * M_HEADS
    o_kr = o_qlat + A_Q_RANK + A_KV_RANK
    o_gm = o_kr + A_ROPE
    zeros = lambda n: jnp.zeros((D, n), w_in.dtype)
    misc = jnp.concatenate([zeros(A_NOPE), w_in[:, o_kr:o_gm], zeros(LANES - A_QK)], axis=1)
    w_b = jnp.concatenate([w_in[:, :o_gate], w_in[:, o_qlat:o_kr], misc], axis=1).astype(BF16)
    w_gt = w_in[:, o_gate:o_qlat].T.astype(BF16)
    w_g = w_in[:, o_gm:].astype(BF16)
    wuq = jnp.pad(w_uq.reshape(A_Q_RANK, A_HEADS, A_QK), ((0, 0), (0, 0), (0, LANES - A_QK)))
    wuq = wuq.reshape(A_Q_RANK, A_HEADS * LANES).astype(BF16)
    wkv = w_ukv.reshape(A_KV_RANK, A_HEADS, A_NOPE + A_V)
    wk = jnp.pad(wkv[:, :, :A_NOPE], ((0, 0), (0, 0), (0, LANES - A_NOPE)))
    wk = wk.reshape(A_KV_RANK, A_HEADS * LANES).astype(BF16)
    wv = wkv[:, :, A_NOPE:].reshape(A_KV_RANK, A_WIDTH).astype(BF16)
    return w_b, w_gt, w_g, wuq, wk, wv


def _rope_rows():
    inv_freq = ROPE_THETA ** (-np.arange(0, A_ROPE, 2, dtype=np.float32) / A_ROPE)
    half = A_ROPE // 2
    invf = np.zeros((1, LANES), np.float32)
    invf[0, A_NOPE:A_NOPE + half] = inv_freq
    invf[0, A_NOPE + half:A_QK] = inv_freq
    sgn = np.zeros((1, LANES), np.float32)
    sgn[0, A_NOPE:A_NOPE + half] = -1.0
    sgn[0, A_NOPE + half:A_QK] = 1.0
    return invf, sgn


def kernel(x, c, positions, w_ada, b_ada, g_mix, w_in, conv_w, conv_b, b_igate, b_fgate, g_mhead,
           g_q_lat, w_uq, g_kv_lat, w_ukv, w_branch_m, w_branch_a, w_out, g_ffn, w_ffn_in,
           w_ffn_out, g_final):
    B, S, D = x.shape
    depth = w_ada.shape[0]
    inv_freq = ROPE_THETA ** (-jnp.arange(0, A_ROPE, 2, dtype=F32) / A_ROPE)
    zpad = lambda n: jnp.zeros((n,), F32)
    invf = jnp.concatenate([zpad(A_NOPE), inv_freq, inv_freq, zpad(LANES - A_QK)]).reshape(1, LANES)
    half = A_ROPE // 2
    sgn = jnp.concatenate([zpad(A_NOPE), -jnp.ones((half,), F32), jnp.ones((half,), F32),
                           zpad(LANES - A_QK)]).reshape(1, LANES)
    pos3 = positions.reshape(B, S, 1)
    h = x
    for l in range(depth):
        ada3 = _ada(c, w_ada[l], b_ada[l]).reshape(B, 6, D)
        w_b, w_gt, w_g, wuq, wk, wv = _pack_weights(w_in[l], w_uq[l], w_ukv[l])
        b_col = jnp.concatenate([b_igate[l], b_fgate[l]]).astype(F32).reshape(2 * M_HEADS, 1)
        gmix = g_mix[l].reshape(1, D)
        qm, km, vm, om, gr, qa, ka, va = _mix_in(
            h, ada3, pos3, gmix, w_b, w_gt, conv_w[l], conv_b[l].reshape(1, -1), b_col,
            g_q_lat[l].reshape(1, -1), g_kv_lat[l].reshape(1, -1), wuq, wk, wv, invf, sgn)
        hm = _mlstm(qm, km, vm, om, gr, g_mhead[l].reshape(1, -1))
        oa = _attn(qa, ka, va)
        h1 = _merge(h, ada3, hm, oa, gmix, w_g, w_branch_m[l].astype(BF16),
                    w_branch_a[l].astype(BF16), w_out[l].astype(BF16))
        last = l == depth - 1
        h = _ffn(h1, ada3, g_ffn[l].reshape(1, D), g_final.reshape(1, D),
                 w_ffn_in[l].astype(BF16), w_ffn_out[l].astype(BF16))
        assert last, "final rmsnorm is fused into the last layer's FFN kernel"
    return h
```

```python
import functools

import jax
import jax.numpy as jnp
import numpy as np
from jax import lax
from jax.experimental import pallas as pl
from jax.experimental.pallas import tpu as pltpu

F32 = jnp.float32
BF16 = jnp.bfloat16

D_MODEL = 1024
M_HEADS = 4
M_HEAD_DIM = 128
M_WIDTH = M_HEADS * M_HEAD_DIM
M_CONV = 4
A_HEADS = 8
A_NOPE = 64
A_ROPE = 32
A_V = 64
A_QK = A_NOPE + A_ROPE
A_Q_RANK = 256
A_KV_RANK = 128
A_WIDTH = A_HEADS * A_V
ROPE_THETA = 10000.0
D_FF = 2816
NORM_EPS = 1e-6

LANES = 128
HALO = 8
VMEM_LIMIT = 56 * 1024 * 1024

MIX_TM = 512
M_CHUNK = 128
ATT_T = 256
LOG2E = 1.4426950408889634
FFN_TM = 512
FFN_TF = 1408


def _sigmoid(x):
    return 1.0 / (1.0 + jnp.exp(-x))


def _silu(x):
    return x * _sigmoid(x)


def _log_sigmoid(x):
    return jnp.minimum(x, 0.0) - jnp.log1p(jnp.exp(-jnp.abs(x)))


def _rms(x, g):
    n, d = x.shape
    ms = jnp.broadcast_to(jnp.sum(x * x, axis=-1, keepdims=True), (n, LANES)) * (1.0 / d)
    r = lax.rsqrt(ms + NORM_EPS)
    if d > LANES:
        r = jnp.concatenate([r] * (d // LANES), axis=1)
    return x * r * g


def _bdot(a, b):
    return jnp.dot(a, b, preferred_element_type=F32)


def _dot_nt(a, b):
    return lax.dot_general(a, b, (((1,), (1,)), ((), ())), preferred_element_type=F32)


def _ada_kernel(c_ref, w_ref, b_ref, o_ref):
    c = c_ref[...]
    o_ref[...] = jnp.dot(_silu(c), w_ref[...], preferred_element_type=F32,
                         precision=lax.Precision.HIGHEST) + b_ref[...]


def _ada(c, w_ada, b_ada):
    B, D = c.shape
    N = w_ada.shape[1]
    return pl.pallas_call(
        _ada_kernel,
        grid=(N // D,),
        in_specs=[pl.BlockSpec((B, D), lambda j: (0, 0)),
                  pl.BlockSpec((D, D), lambda j: (0, j)),
                  pl.BlockSpec((1, D), lambda j: (0, j))],
        out_specs=pl.BlockSpec((B, D), lambda j: (0, j)),
        out_shape=jax.ShapeDtypeStruct((B, N), F32),
        compiler_params=pltpu.CompilerParams(dimension_semantics=("arbitrary",),
                                             vmem_limit_bytes=VMEM_LIMIT),
        name="ada",
    )(c, w_ada, b_ada.reshape(1, N))


def _rope_partner(blk, lane):
    up = pltpu.roll(blk, LANES - A_ROPE // 2, 1)
    dn = pltpu.roll(blk, A_ROPE // 2, 1)
    return jnp.where(lane < A_NOPE + A_ROPE // 2, up, dn)


def _mix_in_kernel(x_ref, xh_ref, ada_ref, pos_ref, gmix_ref, w_ref, wgt_ref, cw_ref, cb_ref,
                   bcol_ref, gq_ref, gkv_ref, wuq_ref, wk_ref, wv_ref, invf_ref, sgn_ref,
                   qm_ref, km_ref, vm_ref, om_ref, gr_ref, qa_ref, ka_ref, va_ref):
    i = pl.program_id(1)
    tm = x_ref.shape[1]
    sh1 = ada_ref[0, 0:1, :]
    sc1 = ada_ref[0, 1:2, :]
    g = gmix_ref[...]

    def modulate(xv):
        return _rms(xv, g) * (1.0 + sc1) + sh1

    u = modulate(x_ref[0]).astype(BF16)
    uh = modulate(xh_ref[0]).astype(BF16)

    wqk = w_ref[:, 0:2 * M_WIDTH]
    z = _bdot(u, wqk)
    zh = _bdot(uh, wqk)
    zh = jnp.where(i > 0, zh, 0.0)
    ext = jnp.concatenate([zh, z], axis=0)
    y = cb_ref[...]
    for j in range(M_CONV):
        sft = M_CONV - 1 - j
        y = y + ext[HALO - sft:HALO - sft + tm] * cw_ref[j:j + 1, :]
    qk = _silu(y)
    qm_ref[0] = (qk[:, :M_WIDTH] * (M_HEAD_DIM ** -0.5)).astype(BF16)
    km_ref[0] = qk[:, M_WIDTH:].astype(BF16)

    vm_ref[0] = _bdot(u, w_ref[:, 2 * M_WIDTH:3 * M_WIDTH]).astype(BF16)
    om_ref[0] = _bdot(u, w_ref[:, 3 * M_WIDTH:4 * M_WIDTH])

    gpre = _dot_nt(wgt_ref[...], u) + bcol_ref[...]
    row = lax.broadcasted_iota(jnp.int32, gpre.shape, 0)
    gates = jnp.where(row < M_HEADS, gpre, _log_sigmoid(gpre))
    for cix in range(tm // M_CHUNK):
        gr_ref[0, 2 * M_HEADS * cix:2 * M_HEADS * (cix + 1), :] = gates[:, cix * M_CHUNK:(cix + 1) * M_CHUNK]

    lat = _bdot(u, w_ref[:, 4 * M_WIDTH:4 * M_WIDTH + 4 * LANES])
    q_lat = lat[:, 0:A_Q_RANK]
    kv_lat = lat[:, A_Q_RANK:A_Q_RANK + A_KV_RANK]
    misc = lat[:, A_Q_RANK + A_KV_RANK:]

    pos = pos_ref[0].astype(F32)
    ang = pos * invf_ref[...]
    cosv = jnp.cos(ang)
    sinv = jnp.sin(ang) * sgn_ref[...]
    lane = lax.broadcasted_iota(jnp.int32, (tm, LANES), 1)

    cq = _rms(q_lat, gq_ref[...]).astype(BF16)
    qa = _bdot(cq, wuq_ref[...])
    scale = A_QK ** -0.5 * LOG2E
    for h in range(A_HEADS):
        blk = qa[:, h * LANES:(h + 1) * LANES]
        rot = blk * cosv + _rope_partner(blk, lane) * sinv
        qa_ref[0, :, h * LANES:(h + 1) * LANES] = (rot * scale).astype(BF16)

    ckv = _rms(kv_lat, gkv_ref[...]).astype(BF16)
    kn = _bdot(ckv, wk_ref[...])
    va_ref[0] = _bdot(ckv, wv_ref[...]).astype(BF16)
    is_rope = (lane >= A_NOPE) & (lane < A_QK)
    krot = jnp.where(is_rope, misc * cosv + _rope_partner(misc, lane) * sinv, 0.0)
    for h in range(A_HEADS):
        ka_ref[0, :, h * LANES:(h + 1) * LANES] = (kn[:, h * LANES:(h + 1) * LANES] + krot).astype(BF16)


def _mix_in(x, ada3, pos3, g_mix, w_b, w_gt, conv_w, conv_b, b_col, g_q, g_kv, wuq, wk, wv,
            invf, sgn):
    B, S, D = x.shape
    tm = min(MIX_TM, S)
    nt = S // tm
    hb = tm // HALO
    const = lambda shape: pl.BlockSpec(shape, lambda b, i: (0,) * len(shape))
    tok = lambda w: pl.BlockSpec((1, tm, w), lambda b, i: (b, i, 0))
    out_shapes = (
        jax.ShapeDtypeStruct((B, S, M_WIDTH), BF16),
        jax.ShapeDtypeStruct((B, S, M_WIDTH), BF16),
        jax.ShapeDtypeStruct((B, S, M_WIDTH), BF16),
        jax.ShapeDtypeStruct((B, S, M_WIDTH), F32),
        jax.ShapeDtypeStruct((B, 2 * M_HEADS * (S // M_CHUNK), M_CHUNK), F32),
        jax.ShapeDtypeStruct((B, S, A_HEADS * LANES), BF16),
        jax.ShapeDtypeStruct((B, S, A_HEADS * LANES), BF16),
        jax.ShapeDtypeStruct((B, S, A_WIDTH), BF16),
    )
    return pl.pallas_call(
        _mix_in_kernel,
        grid=(B, nt),
        in_specs=[tok(D),
                  pl.BlockSpec((1, HALO, D), lambda b, i: (b, jnp.maximum(i * hb - 1, 0), 0)),
                  pl.BlockSpec((1, 6, D), lambda b, i: (b, 0, 0)),
                  tok(1),
                  const((1, D)), const(w_b.shape), const(w_gt.shape), const(conv_w.shape),
                  const(conv_b.shape), const(b_col.shape), const(g_q.shape), const(g_kv.shape),
                  const(wuq.shape), const(wk.shape), const(wv.shape), const(invf.shape),
                  const(sgn.shape)],
        out_specs=(tok(M_WIDTH), tok(M_WIDTH), tok(M_WIDTH), tok(M_WIDTH),
                   pl.BlockSpec((1, 2 * M_HEADS * (tm // M_CHUNK), M_CHUNK), lambda b, i: (b, i, 0)),
                   tok(A_HEADS * LANES), tok(A_HEADS * LANES), tok(A_WIDTH)),
        out_shape=out_shapes,
        compiler_params=pltpu.CompilerParams(dimension_semantics=("parallel", "arbitrary"),
                                             vmem_limit_bytes=VMEM_LIMIT),
        name="mix_in",
    )(x, x, ada3, pos3, g_mix, w_b, w_gt, conv_w, conv_b, b_col, g_q, g_kv, wuq, wk, wv, invf, sgn)


def _split3(x):
    hi = x.astype(BF16)
    r = x - hi.astype(F32)
    mid = r.astype(BF16)
    lo = (r - mid.astype(F32)).astype(BF16)
    return hi, mid, lo


def _mlstm_kernel(q_ref, k_ref, v_ref, om_ref, gr_ref, gmh_ref, uo_ref, o_ref,
                  c_scr, rows_scr, dw_scr, iw_scr, em_scr, wc_scr):
    S = q_ref.shape[1]
    L = M_CHUNK
    dh = M_HEAD_DIM
    H = M_HEADS
    nc = S // L
    R = 2 * H * nc

    g = gr_ref[0]
    prod = _bdot(jnp.concatenate(_split3(g), axis=0), uo_ref[...])
    prod = prod[:R] + prod[R:2 * R] + prod[2 * R:]
    cs = prod[:, :L]
    tot = prod[:, L:]
    li = pltpu.roll(g, H, 0)
    wl = (tot - cs) + li
    wmax = jnp.broadcast_to(jnp.max(wl, axis=-1, keepdims=True), (R, L))
    m_cur = jnp.zeros((2 * H, L), F32)
    m_prev, m_new = [], []
    for c in range(nc):
        rs = slice(2 * H * c, 2 * H * (c + 1))
        m_prev.append(m_cur)
        m_cur = jnp.maximum(tot[rs] + m_cur, wmax[rs])
        m_new.append(m_cur)
    m_prev = jnp.concatenate(m_prev, axis=0)
    m_new = jnp.concatenate(m_new, axis=0)
    rows_scr[0] = cs
    rows_scr[1] = li
    rows_scr[2] = wl
    rows_scr[3] = m_prev
    rows_scr[4] = m_new
    rows_scr[5] = jnp.exp(tot + m_prev - m_new)

    causal = (lax.broadcasted_iota(jnp.int32, (L, L), 0) >= lax.broadcasted_iota(jnp.int32, (L, L), 1))
    upper = lax.broadcasted_iota(jnp.int32, (2 * H, L), 0) < H

    def prep(c, slot):
        rs = pl.ds(pl.multiple_of(c * 2 * H, 2 * H), 2 * H)
        cs_c, li_c, wl_c = rows_scr[0, rs, :], rows_scr[1, rs, :], rows_scr[2, rs, :]
        mp_c, mn_c = rows_scr[3, rs, :], rows_scr[4, rs, :]
        z = jnp.where(upper, pltpu.roll(wl_c, H, 0), cs_c)
        cols = jnp.concatenate([z] * (L // (2 * H)), axis=0).T
        for h in range(H):
            wl_col = jnp.broadcast_to(cols[:, h:h + 1], (L, L))
            b_col = jnp.broadcast_to(cols[:, H + h:H + h + 1], (L, L))
            r = H + h
            dlog = jnp.where(causal, (b_col - cs_c[r:r + 1, :]) + li_c[r:r + 1, :], -jnp.inf)
            inter_log = b_col + mp_c[r:r + 1, :]
            m_t = jnp.maximum(inter_log, jnp.max(dlog, axis=-1, keepdims=True))
            dw_scr[slot, h] = jnp.exp(dlog - m_t)
            iw_scr[slot, h] = jnp.exp(inter_log - m_t)
            em_scr[slot, h] = jnp.exp(-m_t)
            wc_scr[slot, h] = jnp.exp(wl_col - mn_c[r:r + 1, :])

    c_scr[...] = jnp.zeros_like(c_scr)
    ones = jnp.ones((L, dh), BF16)

    def chunk(c, slot):
        ts = pl.ds(pl.multiple_of(c * L, L), L)
        for h in range(H):
            hs = slice(h * dh, (h + 1) * dh)
            q = q_ref[0, ts, hs]
            k = k_ref[0, ts, hs]
            v_aug = jnp.concatenate([v_ref[0, ts, hs], ones], axis=1)
            c_aug = c_scr[h]
            iw = iw_scr[slot, h]
            s = _dot_nt(q, k) * dw_scr[slot, h]
            qc = _bdot(q, c_aug.astype(BF16))
            sv = _bdot(s.astype(BF16), v_aug)
            num = sv[:, :dh] + iw * qc[:, :dh]
            den = sv[:, dh:] + iw * qc[:, dh:]
            hv = num / jnp.maximum(jnp.abs(den), em_scr[slot, h])
            hn = _rms(hv, gmh_ref[:, hs])
            o_ref[0, ts, hs] = (_sigmoid(om_ref[0, ts, hs]) * hn).astype(o_ref.dtype)

            decay = rows_scr[5, pl.ds(c * 2 * H + H + h, 1), :]
            kw_t = (k.astype(F32) * wc_scr[slot, h]).T.astype(BF16)
            c_scr[h] = jnp.concatenate([decay, decay], axis=1) * c_aug + _bdot(kw_t, v_aug)

    prep(0, 0)

    def pair(i, carry):
        c = 2 * i
        chunk(c, 0)
        prep(c + 1, 1)
        chunk(c + 1, 1)
        prep(jnp.minimum(c + 2, nc - 1), 0)
        return carry

    lax.fori_loop(0, nc // 2, pair, 0)


def _mlstm(qm, km, vm, om, gr, g_mhead):
    B, S, W = qm.shape
    L = M_CHUNK
    uo = np.concatenate([np.triu(np.ones((L, L), np.float32)), np.ones((L, L), np.float32)], axis=1)
    seq = lambda: pl.BlockSpec((1, S, W), lambda b: (b, 0, 0))
    rows = 2 * M_HEADS * (S // L)
    assert (S // L) % 2 == 0, "the chunk loop handles two chunks per iteration"
    per_head = lambda: pltpu.VMEM((2, M_HEADS, L, L), F32)
    return pl.pallas_call(
        _mlstm_kernel,
        grid=(B,),
        in_specs=[seq(), seq(), seq(), seq(),
                  pl.BlockSpec((1, rows, L), lambda b: (b, 0, 0)),
                  pl.BlockSpec((1, W), lambda b: (0, 0)),
                  pl.BlockSpec((L, 2 * L), lambda b: (0, 0))],
        out_specs=seq(),
        out_shape=jax.ShapeDtypeStruct((B, S, W), BF16),
        scratch_shapes=[pltpu.VMEM((M_HEADS, M_HEAD_DIM, 2 * M_HEAD_DIM), F32),
                        pltpu.VMEM((6, rows, L), F32),
                        per_head(), per_head(), per_head(), per_head()],
        compiler_params=pltpu.CompilerParams(dimension_semantics=("parallel",),
                                             vmem_limit_bytes=VMEM_LIMIT),
        name="mlstm",
    )(qm, km, vm, om, gr, g_mhead, jnp.asarray(uo, BF16))


def _attn_kernel(q_ref, k_ref, v_ref, o_ref, m_scr, acc_scr, s_scr, p_scr, al_scr):
    S = q_ref.shape[1]
    t = min(ATT_T, S)
    nq = S // t
    causal = (lax.broadcasted_iota(jnp.int32, (t, t), 1)
              <= lax.broadcasted_iota(jnp.int32, (t, t), 0))

    def tile(i):
        return pl.ds(pl.multiple_of(i * t, t), t)

    def head(hh):
        return slice(hh * LANES, (hh + 1) * LANES)

    def scores(blk, slot):
        qi, kt = blk
        for hh in range(2):
            s_scr[slot, hh] = _dot_nt(q_ref[0, tile(qi), head(hh)], k_ref[0, tile(kt), head(hh)])

    def softmax(blk, slot, diag):
        qi, _ = blk
        for hh in range(2):
            mask = (lambda x: jnp.where(causal, x, -jnp.inf)) if diag else (lambda x: x)
            row_max = jnp.max(mask(s_scr[slot, hh]), axis=-1, keepdims=True)
            if diag:
                m_new = jnp.broadcast_to(row_max, (t, LANES))
            else:
                m_old = m_scr[hh, tile(qi), :]
                m_new = jnp.maximum(m_old, row_max)
                al_scr[slot, hh] = jnp.exp2(m_old - m_new)
            m_scr[hh, tile(qi), :] = m_new
            p = jnp.exp2(mask(s_scr[slot, hh]) - jnp.concatenate([m_new] * (t // LANES), axis=1))
            p_scr[slot, hh * t:(hh + 1) * t, :] = p.astype(BF16)

    ones = jnp.ones((t, LANES), BF16)

    def values(blk, slot, diag):
        qi, kt = blk
        pv = _bdot(p_scr[slot], jnp.concatenate([v_ref[0, tile(kt), :], ones], axis=1))
        for hh in range(2):
            part = pv[hh * t:(hh + 1) * t]
            if diag:
                acc_scr[hh, tile(qi), :] = part
            else:
                alpha = al_scr[slot, hh]
                acc_scr[hh, tile(qi), :] = (jnp.concatenate([alpha, alpha], axis=1)
                                            * acc_scr[hh, tile(qi), :] + part)

    def run(n, first, advance, diag):
        if n == 0:
            return

        def step(bc, bb, ba, slot):
            values(bc, slot, diag)
            softmax(bb, 1 - slot, diag)
            scores(ba, slot)

        b0 = first
        scores(b0, 0)
        softmax(b0, 0, diag)
        if n == 1:
            values(b0, 0, diag)
            return
        b1 = advance(*b0)
        scores(b1, 1)

        def body(_, carry):
            bc, bb = carry
            ba = advance(*bb)
            step(bc, bb, ba, 0)
            bz = advance(*ba)
            step(bb, ba, bz, 1)
            return ba, bz

        bc, bb = lax.fori_loop(0, (n - 2) // 2, body, (b0, b1))
        if (n - 2) % 2:
            ba = advance(*bb)
            step(bc, bb, ba, 0)
            bc, bb = bb, ba
        last = (n - 1) % 2
        values(bc, 1 - last, diag)
        softmax(bb, last, diag)
        values(bb, last, diag)

    def next_lower(qi, kt):
        wrap = kt + 1 >= qi
        return jnp.where(wrap, qi + 1, qi), jnp.where(wrap, 0, kt + 1)

    zero, one = jnp.int32(0), jnp.int32(1)
    run(nq, (zero, zero), lambda qi, kt: (qi + 1, kt + 1), True)
    run(nq * (nq - 1) // 2, (one, zero), next_lower, False)

    lane = lax.broadcasted_iota(jnp.int32, (S, LANES), 1)
    outs = [acc_scr[hh, :, :LANES] / acc_scr[hh, :, LANES:] for hh in range(2)]
    o_ref[0] = jnp.where(lane < A_V, outs[0], outs[1]).astype(o_ref.dtype)


def _attn(qa, ka, va):
    B, S, _ = qa.shape
    t = min(ATT_T, S)
    pair = lambda w: pl.BlockSpec((1, S, w), lambda b, p: (b, 0, p))
    return pl.pallas_call(
        _attn_kernel,
        grid=(B, A_HEADS // 2),
        in_specs=[pair(2 * LANES), pair(2 * LANES), pair(LANES)],
        out_specs=pair(LANES),
        out_shape=jax.ShapeDtypeStruct((B, S, A_WIDTH), BF16),
        scratch_shapes=[pltpu.VMEM((2, S, LANES), F32), pltpu.VMEM((2, S, 2 * LANES), F32),
                        pltpu.VMEM((2, 2, t, t), F32), pltpu.VMEM((2, 2 * t, t), BF16),
                        pltpu.VMEM((2, 2, t, LANES), F32)],
        compiler_params=pltpu.CompilerParams(dimension_semantics=("parallel", "parallel"),
                                             vmem_limit_bytes=VMEM_LIMIT),
        name="attn",
    )(qa, ka, va)


def _merge_kernel(x_ref, ada_ref, hm_ref, oa_ref, gmix_ref, wg_ref, wbm_ref, wba_ref, wo_ref, h_ref):
    x = x_ref[0]
    sh1 = ada_ref[0, 0:1, :]
    sc1 = ada_ref[0, 1:2, :]
    gt1 = ada_ref[0, 2:3, :]
    u = (_rms(x, gmix_ref[...]) * (1.0 + sc1) + sh1).astype(BF16)
    D = x.shape[1]
    y = _sigmoid(_bdot(u, wg_ref[:, 0:D])) * _bdot(hm_ref[0], wbm_ref[...])
    y = y + _sigmoid(_bdot(u, wg_ref[:, D:2 * D])) * _bdot(oa_ref[0], wba_ref[...])
    h_ref[0] = x + gt1 * _bdot(y.astype(BF16), wo_ref[...])


def _merge(x, ada3, hm, oa, g_mix, w_g, w_bm, w_ba, w_o):
    B, S, D = x.shape
    tm = min(MIX_TM, S)
    const = lambda shape: pl.BlockSpec(shape, lambda b, i: (0,) * len(shape))
    tok = lambda w: pl.BlockSpec((1, tm, w), lambda b, i: (b, i, 0))
    return pl.pallas_call(
        _merge_kernel,
        grid=(B, S // tm),
        in_specs=[tok(D), pl.BlockSpec((1, 6, D), lambda b, i: (b, 0, 0)), tok(M_WIDTH), tok(A_WIDTH),
                  const((1, D)), const(w_g.shape), const(w_bm.shape), const(w_ba.shape),
                  const(w_o.shape)],
        out_specs=tok(D),
        out_shape=jax.ShapeDtypeStruct((B, S, D), F32),
        compiler_params=pltpu.CompilerParams(dimension_semantics=("parallel", "parallel"),
                                             vmem_limit_bytes=VMEM_LIMIT),
        name="merge",
    )(x, ada3, hm, oa, g_mix, w_g, w_bm, w_ba, w_o)


def _ffn_kernel(h_ref, ada_ref, gffn_ref, gfin_ref, wi_ref, wo_ref, o_ref):
    h = h_ref[0]
    sh2 = ada_ref[0, 3:4, :]
    sc2 = ada_ref[0, 4:5, :]
    gt2 = ada_ref[0, 5:6, :]
    u = (_rms(h, gffn_ref[...]) * (1.0 + sc2) + sh2).astype(BF16)
    acc = None
    for j in range(D_FF // FFN_TF):
        lo = j * FFN_TF
        a = _silu(_bdot(u, wi_ref[:, lo:lo + FFN_TF])) * _bdot(u, wi_ref[:, D_FF + lo:D_FF + lo + FFN_TF])
        part = _bdot(a.astype(BF16), wo_ref[lo:lo + FFN_TF, :])
        acc = part if acc is None else acc + part
    o_ref[0] = _rms(h + gt2 * acc, gfin_ref[...])


def _ffn(h1, ada3, g_ffn, g_final, w_in, w_out):
    B, S, D = h1.shape
    tm = min(FFN_TM, S)
    tok = pl.BlockSpec((1, tm, D), lambda b, i: (b, i, 0))
    resident = lambda shape: pl.BlockSpec(shape, lambda b, i: (0, 0), pipeline_mode=pl.Buffered(1))
    return pl.pallas_call(
        _ffn_kernel,
        grid=(B, S // tm),
        in_specs=[tok,
                  pl.BlockSpec((1, 6, D), lambda b, i: (b, 0, 0)),
                  resident((1, D)), resident((1, D)), resident(w_in.shape), resident(w_out.shape)],
        out_specs=tok,
        out_shape=jax.ShapeDtypeStruct((B, S, D), F32),
        compiler_params=pltpu.CompilerParams(dimension_semantics=("parallel", "parallel"),
                                             vmem_limit_bytes=VMEM_LIMIT),
        name="ffn",
    )(h1, ada3, g_ffn, g_final, w_in, w_out)


def _pack_weights(w_in, w_uq, w_ukv):
    D = w_in.shape[0]
    o_gate = 4 * M_WIDTH
    o_qlat = o_gate + 2 * M_HEADS
    o_kr = o_qlat + A_Q_RANK + A_KV_RANK
    o_gm = o_kr + A_ROPE
    zeros = lambda n: jnp.zeros((D, n), w_in.dtype)
    misc = jnp.concatenate([zeros(A_NOPE), w_in[:, o_kr:o_gm], zeros(LANES - A_QK)], axis=1)
    w_b = jnp.concatenate([w_in[:, :o_gate], w_in[:, o_qlat:o_kr], misc], axis=1).astype(BF16)
    w_gt = w_in[:, o_gate:o_qlat].T.astype(BF16)
    w_g = w_in[:, o_gm:].astype(BF16)
    wuq = jnp.pad(w_uq.reshape(A_Q_RANK, A_HEADS, A_QK), ((0, 0), (0, 0), (0, LANES - A_QK)))
    wuq = wuq.reshape(A_Q_RANK, A_HEADS * LANES).astype(BF16)
    wkv = w_ukv.reshape(A_KV_RANK, A_HEADS, A_NOPE + A_V)
    wk = jnp.pad(wkv[:, :, :A_NOPE], ((0, 0), (0, 0), (0, LANES - A_NOPE)))
    wk = wk.reshape(A_KV_RANK, A_HEADS * LANES).astype(BF16)
    wv = wkv[:, :, A_NOPE:].reshape(A_KV_RANK, A_WIDTH).astype(BF16)
    return w_b, w_gt, w_g, wuq, wk, wv


def _rope_rows():
    inv_freq = ROPE_THETA ** (-np.arange(0, A_ROPE, 2, dtype=np.float32) / A_ROPE)
    half = A_ROPE // 2
    invf = np.zeros((1, LANES), np.float32)
    invf[0, A_NOPE:A_NOPE + half] = inv_freq
    invf[0, A_NOPE + half:A_QK] = inv_freq
    sgn = np.zeros((1, LANES), np.float32)
    sgn[0, A_NOPE:A_NOPE + half] = -1.0
    sgn[0, A_NOPE + half:A_QK] = 1.0
    return invf, sgn


def kernel(x, c, positions, w_ada, b_ada, g_mix, w_in, conv_w, conv_b, b_igate, b_fgate, g_mhead,
           g_q_lat, w_uq, g_kv_lat, w_ukv, w_branch_m, w_branch_a, w_out, g_ffn, w_ffn_in,
           w_ffn_out, g_final):
    B, S, D = x.shape
    depth = w_ada.shape[0]
    inv_freq = ROPE_THETA ** (-jnp.arange(0, A_ROPE, 2, dtype=F32) / A_ROPE)
    zpad = lambda n: jnp.zeros((n,), F32)
    invf = jnp.concatenate([zpad(A_NOPE), inv_freq, inv_freq, zpad(LANES - A_QK)]).reshape(1, LANES)
    half = A_ROPE // 2
    sgn = jnp.concatenate([zpad(A_NOPE), -jnp.ones((half,), F32), jnp.ones((half,), F32),
                           zpad(LANES - A_QK)]).reshape(1, LANES)
    pos3 = positions.reshape(B, S, 1)
    h = x
    for l in range(depth):
        ada3 = _ada(c, w_ada[l], b_ada[l]).reshape(B, 6, D)
        w_b, w_gt, w_g, wuq, wk, wv = _pack_weights(w_in[l], w_uq[l], w_ukv[l])
        b_col = jnp.concatenate([b_igate[l], b_fgate[l]]).astype(F32).reshape(2 * M_HEADS, 1)
        gmix = g_mix[l].reshape(1, D)
        qm, km, vm, om, gr, qa, ka, va = _mix_in(
            h, ada3, pos3, gmix, w_b, w_gt, conv_w[l], conv_b[l].reshape(1, -1), b_col,
            g_q_lat[l].reshape(1, -1), g_kv_lat[l].reshape(1, -1), wuq, wk, wv, invf, sgn)
        hm = _mlstm(qm, km, vm, om, gr, g_mhead[l].reshape(1, -1))
        oa = _attn(qa, ka, va)
        h1 = _merge(h, ada3, hm, oa, gmix, w_g, w_branch_m[l].astype(BF16),
                    w_branch_a[l].astype(BF16), w_out[l].astype(BF16))
        last = l == depth - 1
        h = _ffn(h1, ada3, g_ffn[l].reshape(1, D), g_final.reshape(1, D),
                 w_ffn_in[l].astype(BF16), w_ffn_out[l].astype(BF16))
        assert last, "final rmsnorm is fused into the last layer's FFN kernel"
    return h
```

```python
import functools

import jax
import jax.numpy as jnp
import numpy as np
from jax import lax
from jax.experimental import pallas as pl
from jax.experimental.pallas import tpu as pltpu

F32 = jnp.float32
BF16 = jnp.bfloat16

D_MODEL = 1024
M_HEADS = 4
M_HEAD_DIM = 128
M_WIDTH = M_HEADS * M_HEAD_DIM
M_CONV = 4
A_HEADS = 8
A_NOPE = 64
A_ROPE = 32
A_V = 64
A_QK = A_NOPE + A_ROPE
A_Q_RANK = 256
A_KV_RANK = 128
A_WIDTH = A_HEADS * A_V
ROPE_THETA = 10000.0
D_FF = 2816
NORM_EPS = 1e-6

LANES = 128
SUBLANES = 8
HALO = 8
VMEM_LIMIT = 56 * 1024 * 1024

MIX_TM = 512
MIX_TN = 256
M_CHUNK = 128
ATT_T = 256
LOG2E = 1.4426950408889634
FFN_TM = 512
FFN_TF = 1408


def _sigmoid(x):
    return 1.0 / (1.0 + jnp.exp(-x))


def _silu(x):
    return x * _sigmoid(x)


def _log_sigmoid(x):
    return jnp.minimum(x, 0.0) - jnp.log1p(jnp.exp(-jnp.abs(x)))


def _rows(v8, n):
    return v8 if n == SUBLANES else jnp.concatenate([v8] * (n // SUBLANES), axis=0)


def _rms(x, g8):
    n, d = x.shape
    ms = jnp.broadcast_to(jnp.sum(x * x, axis=-1, keepdims=True), (n, LANES)) * (1.0 / d)
    r = lax.rsqrt(ms + NORM_EPS)
    if d > LANES:
        r = jnp.concatenate([r] * (d // LANES), axis=1)
    return x * r * _rows(g8, n)


def _ada_rows(ada_ref, k):
    return ada_ref[0, SUBLANES * k:SUBLANES * (k + 1), :]


def _bdot(a, b):
    return jnp.dot(a, b, preferred_element_type=F32)


def _dot_nt(a, b):
    return lax.dot_general(a, b, (((1,), (1,)), ((), ())), preferred_element_type=F32)


def _ada_kernel(c_ref, w_ref, b_ref, o_ref):
    c = c_ref[...]
    o_ref[...] = jnp.dot(_silu(c), w_ref[...], preferred_element_type=F32,
                         precision=lax.Precision.HIGHEST) + b_ref[...]


def _ada(c, w_ada, b_ada):
    B, D = c.shape
    N = w_ada.shape[1]
    return pl.pallas_call(
        _ada_kernel,
        grid=(N // D,),
        in_specs=[pl.BlockSpec((B, D), lambda j: (0, 0)),
                  pl.BlockSpec((D, D), lambda j: (0, j)),
                  pl.BlockSpec((1, D), lambda j: (0, j))],
        out_specs=pl.BlockSpec((B, D), lambda j: (0, j)),
        out_shape=jax.ShapeDtypeStruct((B, N), F32),
        compiler_params=pltpu.CompilerParams(dimension_semantics=("arbitrary",),
                                             vmem_limit_bytes=VMEM_LIMIT),
        name="ada",
    )(c, w_ada, b_ada.reshape(1, N))


def _rope_partner(blk):
    return pltpu.roll(blk, LANES - A_ROPE // 2, 1)


def _mix_in_kernel(x_ref, xh_ref, ada_ref, pos_ref, gmix_ref, w_ref, wgt_ref, cw_ref, cb_ref,
                   bcol_ref, gq_ref, gkv_ref, wuq_ref, wk_ref, wv_ref, invf_ref, sgn_ref,
                   qm_ref, km_ref, vm_ref, om_ref, gr_ref, qa_ref, ka_ref, va_ref, z_scr):
    i = pl.program_id(1)
    tm = x_ref.shape[1]
    sh1 = _ada_rows(ada_ref, 0)
    gain = gmix_ref[...] * (1.0 + _ada_rows(ada_ref, 1))

    def modulate(xv):
        return _rms(xv, gain) + _rows(sh1, xv.shape[0])

    u = modulate(x_ref[0]).astype(BF16)
    uh = modulate(xh_ref[0]).astype(BF16)

    lat = _bdot(u, w_ref[:, 4 * M_WIDTH:4 * M_WIDTH + 4 * LANES])
    q_lat = lat[:, 0:A_Q_RANK]
    kv_lat = lat[:, A_Q_RANK:A_Q_RANK + A_KV_RANK]
    misc = lat[:, A_Q_RANK + A_KV_RANK:]

    pos = pos_ref[0].astype(F32)
    groups = tm // LANES
    ang = None
    for gi in range(groups):
        term = (jnp.broadcast_to(pos[gi * LANES:(gi + 1) * LANES], (LANES, LANES))
                * _rows(invf_ref[SUBLANES * gi:SUBLANES * (gi + 1), :], LANES))
        ang = term if ang is None else ang + term
    cos_p = jnp.cos(ang)
    sin_p = jnp.sin(ang) * _rows(sgn_ref[...], LANES)
    lane_g = lax.broadcasted_iota(jnp.int32, (LANES, LANES), 1)
    rope_g = (lane_g >= A_NOPE) & (lane_g < A_QK)
    cos_parts, sin_parts = [], []
    for gi in range(groups):
        shift = (A_NOPE - A_ROPE * gi) % LANES
        cg = pltpu.roll(cos_p, shift, 1) if shift else cos_p
        sg = pltpu.roll(sin_p, shift, 1) if shift else sin_p
        cos_parts.append(jnp.where(rope_g, cg, 1.0))
        sin_parts.append(jnp.where(rope_g, sg, 0.0))
    cosv = jnp.concatenate(cos_parts, axis=0)
    sinv = jnp.concatenate(sin_parts, axis=0)
    lane = lax.broadcasted_iota(jnp.int32, (tm, LANES), 1)
    is_rope = (lane >= A_NOPE) & (lane < A_QK)

    cq = _rms(q_lat, gq_ref[...]).astype(BF16)
    ckv = _rms(kv_lat, gkv_ref[...]).astype(BF16)
    va_ref[0] = _bdot(ckv, wv_ref[...]).astype(BF16)
    krot = jnp.where(is_rope, misc * cosv + _rope_partner(misc) * sinv, 0.0)
    scale = A_QK ** -0.5 * LOG2E

    assert 2 * M_WIDTH // MIX_TN == A_HEADS // 2
    for cblk in range(2 * M_WIDTH // MIX_TN):
        cols = slice(cblk * MIX_TN, (cblk + 1) * MIX_TN)
        z_scr[0:HALO, cols] = jnp.where(i > 0, _bdot(uh, w_ref[:, cols]), 0.0)
        z_scr[HALO:HALO + tm, cols] = _bdot(u, w_ref[:, cols])
        y = _rows(cb_ref[:, cols], tm)
        for j in range(M_CONV):
            sft = M_CONV - 1 - j
            y = y + (z_scr[HALO - sft:HALO - sft + tm, cols]
                     * _rows(cw_ref[SUBLANES * j:SUBLANES * (j + 1), cols], tm))
        qk = _silu(y)
        if cblk < M_WIDTH // MIX_TN:
            qm_ref[0, :, cols] = (qk * (M_HEAD_DIM ** -0.5)).astype(BF16)
        else:
            km_ref[0, :, cblk * MIX_TN - M_WIDTH:(cblk + 1) * MIX_TN - M_WIDTH] = qk.astype(BF16)

        qa = _bdot(cq, wuq_ref[:, cols])
        kn = _bdot(ckv, wk_ref[:, cols])
        for hh in range(2):
            hcols = slice((2 * cblk + hh) * LANES, (2 * cblk + hh + 1) * LANES)
            blk = qa[:, hh * LANES:(hh + 1) * LANES]
            rot = blk * cosv + _rope_partner(blk) * sinv
            qa_ref[0, :, hcols] = (rot * scale).astype(BF16)
            ka_ref[0, :, hcols] = (kn[:, hh * LANES:(hh + 1) * LANES] + krot).astype(BF16)

        vo = _bdot(u, w_ref[:, 2 * M_WIDTH + cblk * MIX_TN:2 * M_WIDTH + (cblk + 1) * MIX_TN])
        if cblk < M_WIDTH // MIX_TN:
            vm_ref[0, :, cols] = vo.astype(BF16)
        else:
            om_ref[0, :, cblk * MIX_TN - M_WIDTH:(cblk + 1) * MIX_TN - M_WIDTH] = vo

    gpre = _dot_nt(wgt_ref[...], u) + bcol_ref[...]
    row = lax.broadcasted_iota(jnp.int32, gpre.shape, 0)
    gates = jnp.where(row < M_HEADS, gpre, _log_sigmoid(gpre))
    for cix in range(tm // M_CHUNK):
        gr_ref[0, 2 * M_HEADS * cix:2 * M_HEADS * (cix + 1), :] = gates[:, cix * M_CHUNK:(cix + 1) * M_CHUNK]


def _mix_in(x, ada8, pos3, g_mix, w_b, w_gt, conv_w, conv_b, b_col, g_q, g_kv, wuq, wk, wv,
            invf, sgn):
    B, S, D = x.shape
    tm = min(MIX_TM, S)
    nt = S // tm
    hb = tm // HALO
    const = lambda shape: pl.BlockSpec(shape, lambda b, i: (0,) * len(shape))
    tok = lambda w: pl.BlockSpec((1, tm, w), lambda b, i: (b, i, 0))
    out_shapes = (
        jax.ShapeDtypeStruct((B, S, M_WIDTH), BF16),
        jax.ShapeDtypeStruct((B, S, M_WIDTH), BF16),
        jax.ShapeDtypeStruct((B, S, M_WIDTH), BF16),
        jax.ShapeDtypeStruct((B, S, M_WIDTH), F32),
        jax.ShapeDtypeStruct((B, 2 * M_HEADS * (S // M_CHUNK), M_CHUNK), F32),
        jax.ShapeDtypeStruct((B, S, A_HEADS * LANES), BF16),
        jax.ShapeDtypeStruct((B, S, A_HEADS * LANES), BF16),
        jax.ShapeDtypeStruct((B, S, A_WIDTH), BF16),
    )
    return pl.pallas_call(
        _mix_in_kernel,
        grid=(B, nt),
        in_specs=[tok(D),
                  pl.BlockSpec((1, HALO, D), lambda b, i: (b, jnp.maximum(i * hb - 1, 0), 0)),
                  pl.BlockSpec((1,) + ada8.shape[1:], lambda b, i: (b, 0, 0)),
                  tok(1),
                  const(g_mix.shape), const(w_b.shape), const(w_gt.shape), const(conv_w.shape),
                  const(conv_b.shape), const(b_col.shape), const(g_q.shape), const(g_kv.shape),
                  const(wuq.shape), const(wk.shape), const(wv.shape), const(invf.shape),
                  const(sgn.shape)],
        out_specs=(tok(M_WIDTH), tok(M_WIDTH), tok(M_WIDTH), tok(M_WIDTH),
                   pl.BlockSpec((1, 2 * M_HEADS * (tm // M_CHUNK), M_CHUNK), lambda b, i: (b, i, 0)),
                   tok(A_HEADS * LANES), tok(A_HEADS * LANES), tok(A_WIDTH)),
        out_shape=out_shapes,
        scratch_shapes=[pltpu.VMEM((HALO + tm, 2 * M_WIDTH), F32)],
        compiler_params=pltpu.CompilerParams(dimension_semantics=("parallel", "arbitrary"),
                                             vmem_limit_bytes=VMEM_LIMIT),
        name="mix_in",
    )(x, x, ada8, pos3, g_mix, w_b, w_gt, conv_w, conv_b, b_col, g_q, g_kv, wuq, wk, wv, invf, sgn)


def _split3(x):
    hi = x.astype(BF16)
    r = x - hi.astype(F32)
    mid = r.astype(BF16)
    lo = (r - mid.astype(F32)).astype(BF16)
    return hi, mid, lo


def _mlstm_kernel(q_ref, k_ref, v_ref, om_ref, gr_ref, gmh_ref, uo_ref, o_ref,
                  c_scr, rows_scr, dw_scr, iw_scr, em_scr, wc_scr):
    S = q_ref.shape[1]
    L = M_CHUNK
    dh = M_HEAD_DIM
    H = M_HEADS
    nc = S // L
    R = 2 * H * nc

    g = gr_ref[0]
    prod = _bdot(jnp.concatenate(_split3(g), axis=0), uo_ref[...])
    prod = prod[:R] + prod[R:2 * R] + prod[2 * R:]
    cs = prod[:, :L]
    tot = prod[:, L:]
    li = pltpu.roll(g, H, 0)
    wl = (tot - cs) + li
    wmax = jnp.broadcast_to(jnp.max(wl, axis=-1, keepdims=True), (R, L))
    m_cur = jnp.zeros((2 * H, L), F32)
    m_prev, m_new = [], []
    for c in range(nc):
        rs = slice(2 * H * c, 2 * H * (c + 1))
        m_prev.append(m_cur)
        m_cur = jnp.maximum(tot[rs] + m_cur, wmax[rs])
        m_new.append(m_cur)
    m_prev = jnp.concatenate(m_prev, axis=0)
    m_new = jnp.concatenate(m_new, axis=0)
    rows_scr[0] = cs
    rows_scr[1] = li
    rows_scr[2] = wl
    rows_scr[3] = m_prev
    rows_scr[4] = m_new
    rows_scr[5] = jnp.exp(tot + m_prev - m_new)

    causal = (lax.broadcasted_iota(jnp.int32, (L, L), 0) >= lax.broadcasted_iota(jnp.int32, (L, L), 1))
    upper = lax.broadcasted_iota(jnp.int32, (2 * H, L), 0) < H

    def prep(c, slot):
        rs = pl.ds(pl.multiple_of(c * 2 * H, 2 * H), 2 * H)
        cs_c, li_c, wl_c = rows_scr[0, rs, :], rows_scr[1, rs, :], rows_scr[2, rs, :]
        mp_c, mn_c = rows_scr[3, rs, :], rows_scr[4, rs, :]
        z = jnp.where(upper, pltpu.roll(wl_c, H, 0), cs_c)
        cols = jnp.concatenate([z] * (L // (2 * H)), axis=0).T
        for h in range(H):
            wl_col = jnp.broadcast_to(cols[:, h:h + 1], (L, L))
            b_col = jnp.broadcast_to(cols[:, H + h:H + h + 1], (L, L))
            r = H + h
            dlog = jnp.where(causal, (b_col - cs_c[r:r + 1, :]) + li_c[r:r + 1, :], -jnp.inf)
            inter_log = b_col + mp_c[r:r + 1, :]
            m_t = jnp.maximum(inter_log, jnp.max(dlog, axis=-1, keepdims=True))
            dw_scr[slot, h] = jnp.exp(dlog - m_t)
            iw_scr[slot, h] = jnp.exp(inter_log - m_t)
            em_scr[slot, h] = jnp.exp(-m_t)
            wc_scr[slot, h] = jnp.exp(wl_col - mn_c[r:r + 1, :])

    c_scr[...] = jnp.zeros_like(c_scr)
    ones = jnp.ones((L, dh), BF16)

    def chunk(c, slot):
        ts = pl.ds(pl.multiple_of(c * L, L), L)
        for h in range(H):
            hs = slice(h * dh, (h + 1) * dh)
            q = q_ref[0, ts, hs]
            k = k_ref[0, ts, hs]
            v_aug = jnp.concatenate([v_ref[0, ts, hs], ones], axis=1)
            c_aug = c_scr[h]
            iw = iw_scr[slot, h]
            s = _dot_nt(q, k) * dw_scr[slot, h]
            qc = _bdot(q, c_aug.astype(BF16))
            sv = _bdot(s.astype(BF16), v_aug)
            num = sv[:, :dh] + iw * qc[:, :dh]
            den = sv[:, dh:] + iw * qc[:, dh:]
            hv = num / jnp.maximum(jnp.abs(den), em_scr[slot, h])
            hn = _rms(hv, gmh_ref[:, hs])
            o_ref[0, ts, hs] = (_sigmoid(om_ref[0, ts, hs]) * hn).astype(o_ref.dtype)

            decay = rows_scr[5, pl.ds(c * 2 * H + H + h, 1), :]
            kw_t = (k.astype(F32) * wc_scr[slot, h]).T.astype(BF16)
            c_scr[h] = jnp.concatenate([decay, decay], axis=1) * c_aug + _bdot(kw_t, v_aug)

    prep(0, 0)

    def pair(i, carry):
        c = 2 * i
        chunk(c, 0)
        prep(c + 1, 1)
        chunk(c + 1, 1)
        prep(jnp.minimum(c + 2, nc - 1), 0)
        return carry

    lax.fori_loop(0, nc // 2, pair, 0)


def _mlstm(qm, km, vm, om, gr, g_mhead):
    B, S, W = qm.shape
    L = M_CHUNK
    uo = np.concatenate([np.triu(np.ones((L, L), np.float32)), np.ones((L, L), np.float32)], axis=1)
    seq = lambda: pl.BlockSpec((1, S, W), lambda b: (b, 0, 0))
    rows = 2 * M_HEADS * (S // L)
    assert (S // L) % 2 == 0, "the chunk loop handles two chunks per iteration"
    per_head = lambda: pltpu.VMEM((2, M_HEADS, L, L), F32)
    return pl.pallas_call(
        _mlstm_kernel,
        grid=(B,),
        in_specs=[seq(), seq(), seq(), seq(),
                  pl.BlockSpec((1, rows, L), lambda b: (b, 0, 0)),
                  pl.BlockSpec((SUBLANES, W), lambda b: (0, 0)),
                  pl.BlockSpec((L, 2 * L), lambda b: (0, 0))],
        out_specs=seq(),
        out_shape=jax.ShapeDtypeStruct((B, S, W), BF16),
        scratch_shapes=[pltpu.VMEM((M_HEADS, M_HEAD_DIM, 2 * M_HEAD_DIM), F32),
                        pltpu.VMEM((6, rows, L), F32),
                        per_head(), per_head(), per_head(), per_head()],
        compiler_params=pltpu.CompilerParams(dimension_semantics=("parallel",),
                                             vmem_limit_bytes=VMEM_LIMIT),
        name="mlstm",
    )(qm, km, vm, om, gr, g_mhead, jnp.asarray(uo, BF16))


def _attn_kernel(q_ref, k_ref, v_ref, o_ref, m_scr, acc_scr, s_scr, p_scr, al_scr):
    S = q_ref.shape[1]
    t = min(ATT_T, S)
    nq = S // t
    causal = (lax.broadcasted_iota(jnp.int32, (t, t), 1)
              <= lax.broadcasted_iota(jnp.int32, (t, t), 0))

    def tile(i):
        return pl.ds(pl.multiple_of(i * t, t), t)

    def head(hh):
        return slice(hh * LANES, (hh + 1) * LANES)

    def scores(blk, slot):
        qi, kt = blk
        for hh in range(2):
            s_scr[slot, hh] = _dot_nt(q_ref[0, tile(qi), head(hh)], k_ref[0, tile(kt), head(hh)])

    def softmax(blk, slot, diag):
        qi, _ = blk
        for hh in range(2):
            mask = (lambda x: jnp.where(causal, x, -jnp.inf)) if diag else (lambda x: x)
            row_max = jnp.max(mask(s_scr[slot, hh]), axis=-1, keepdims=True)
            if diag:
                m_new = jnp.broadcast_to(row_max, (t, LANES))
            else:
                m_old = m_scr[hh, tile(qi), :]
                m_new = jnp.maximum(m_old, row_max)
                al_scr[slot, hh] = jnp.exp2(m_old - m_new)
            m_scr[hh, tile(qi), :] = m_new
            p = jnp.exp2(mask(s_scr[slot, hh]) - jnp.concatenate([m_new] * (t // LANES), axis=1))
            p_scr[slot, hh * t:(hh + 1) * t, :] = p.astype(BF16)

    ones = jnp.ones((t, LANES), BF16)

    def values(blk, slot, diag):
        qi, kt = blk
        pv = _bdot(p_scr[slot], jnp.concatenate([v_ref[0, tile(kt), :], ones], axis=1))
        for hh in range(2):
            part = pv[hh * t:(hh + 1) * t]
            if diag:
                acc_scr[hh, tile(qi), :] = part
            else:
                alpha = al_scr[slot, hh]
                acc_scr[hh, tile(qi), :] = (jnp.concatenate([alpha, alpha], axis=1)
                                            * acc_scr[hh, tile(qi), :] + part)

    def run(n, first, advance, diag):
        if n == 0:
            return

        def step(bc, bb, ba, slot):
            values(bc, slot, diag)
            softmax(bb, 1 - slot, diag)
            scores(ba, slot)

        b0 = first
        scores(b0, 0)
        softmax(b0, 0, diag)
        if n == 1:
            values(b0, 0, diag)
            return
        b1 = advance(*b0)
        scores(b1, 1)

        def body(_, carry):
            bc, bb = carry
            ba = advance(*bb)
            step(bc, bb, ba, 0)
            bz = advance(*ba)
            step(bb, ba, bz, 1)
            return ba, bz

        bc, bb = lax.fori_loop(0, (n - 2) // 2, body, (b0, b1))
        if (n - 2) % 2:
            ba = advance(*bb)
            step(bc, bb, ba, 0)
            bc, bb = bb, ba
        last = (n - 1) % 2
        values(bc, 1 - last, diag)
        softmax(bb, last, diag)
        values(bb, last, diag)

    def next_lower(qi, kt):
        wrap = kt + 1 >= qi
        return jnp.where(wrap, qi + 1, qi), jnp.where(wrap, 0, kt + 1)

    zero, one = jnp.int32(0), jnp.int32(1)
    run(nq, (zero, zero), lambda qi, kt: (qi + 1, kt + 1), True)
    run(nq * (nq - 1) // 2, (one, zero), next_lower, False)

    lane = lax.broadcasted_iota(jnp.int32, (S, LANES), 1)
    outs = [acc_scr[hh, :, :LANES] / acc_scr[hh, :, LANES:] for hh in range(2)]
    o_ref[0] = jnp.where(lane < A_V, outs[0], outs[1]).astype(o_ref.dtype)


def _attn(qa, ka, va):
    B, S, _ = qa.shape
    t = min(ATT_T, S)
    pair = lambda w: pl.BlockSpec((1, S, w), lambda b, p: (b, 0, p))
    return pl.pallas_call(
        _attn_kernel,
        grid=(B, A_HEADS // 2),
        in_specs=[pair(2 * LANES), pair(2 * LANES), pair(LANES)],
        out_specs=pair(LANES),
        out_shape=jax.ShapeDtypeStruct((B, S, A_WIDTH), BF16),
        scratch_shapes=[pltpu.VMEM((2, S, LANES), F32), pltpu.VMEM((2, S, 2 * LANES), F32),
                        pltpu.VMEM((2, 2, t, t), F32), pltpu.VMEM((2, 2 * t, t), BF16),
                        pltpu.VMEM((2, 2, t, LANES), F32)],
        compiler_params=pltpu.CompilerParams(dimension_semantics=("parallel", "parallel"),
                                             vmem_limit_bytes=VMEM_LIMIT),
        name="attn",
    )(qa, ka, va)


def _merge_kernel(x_ref, ada_ref, hm_ref, oa_ref, gmix_ref, wg_ref, wbm_ref, wba_ref, wo_ref, h_ref):
    x = x_ref[0]
    tm, D = x.shape
    gain = gmix_ref[...] * (1.0 + _ada_rows(ada_ref, 1))
    u = (_rms(x, gain) + _rows(_ada_rows(ada_ref, 0), tm)).astype(BF16)
    y = _sigmoid(_bdot(u, wg_ref[:, 0:D])) * _bdot(hm_ref[0], wbm_ref[...])
    y = y + _sigmoid(_bdot(u, wg_ref[:, D:2 * D])) * _bdot(oa_ref[0], wba_ref[...])
    h_ref[0] = x + _rows(_ada_rows(ada_ref, 2), tm) * _bdot(y.astype(BF16), wo_ref[...])


def _merge(x, ada8, hm, oa, g_mix, w_g, w_bm, w_ba, w_o):
    B, S, D = x.shape
    tm = min(MIX_TM, S)
    const = lambda shape: pl.BlockSpec(shape, lambda b, i: (0,) * len(shape))
    tok = lambda w: pl.BlockSpec((1, tm, w), lambda b, i: (b, i, 0))
    return pl.pallas_call(
        _merge_kernel,
        grid=(B, S // tm),
        in_specs=[tok(D), pl.BlockSpec((1,) + ada8.shape[1:], lambda b, i: (b, 0, 0)), tok(M_WIDTH), tok(A_WIDTH),
                  const(g_mix.shape), const(w_g.shape), const(w_bm.shape), const(w_ba.shape),
                  const(w_o.shape)],
        out_specs=tok(D),
        out_shape=jax.ShapeDtypeStruct((B, S, D), F32),
        compiler_params=pltpu.CompilerParams(dimension_semantics=("parallel", "parallel"),
                                             vmem_limit_bytes=VMEM_LIMIT),
        name="merge",
    )(x, ada8, hm, oa, g_mix, w_g, w_bm, w_ba, w_o)


def _ffn_kernel(h_ref, ada_ref, gffn_ref, gfin_ref, wi_ref, wo_ref, o_ref):
    h = h_ref[0]
    tm = h.shape[0]
    gain = gffn_ref[...] * (1.0 + _ada_rows(ada_ref, 4))
    u = (_rms(h, gain) + _rows(_ada_rows(ada_ref, 3), tm)).astype(BF16)
    acc = None
    for j in range(D_FF // FFN_TF):
        lo = j * FFN_TF
        a = _silu(_bdot(u, wi_ref[:, lo:lo + FFN_TF])) * _bdot(u, wi_ref[:, D_FF + lo:D_FF + lo + FFN_TF])
        part = _bdot(a.astype(BF16), wo_ref[lo:lo + FFN_TF, :])
        acc = part if acc is None else acc + part
    o_ref[0] = _rms(h + _rows(_ada_rows(ada_ref, 5), tm) * acc, gfin_ref[...])


def _ffn(h1, ada8, g_ffn, g_final, w_in, w_out):
    B, S, D = h1.shape
    tm = min(FFN_TM, S)
    tok = pl.BlockSpec((1, tm, D), lambda b, i: (b, i, 0))
    resident = lambda shape: pl.BlockSpec(shape, lambda b, i: (0, 0), pipeline_mode=pl.Buffered(1))
    return pl.pallas_call(
        _ffn_kernel,
        grid=(B, S // tm),
        in_specs=[tok,
                  pl.BlockSpec((1,) + ada8.shape[1:], lambda b, i: (b, 0, 0)),
                  resident(g_ffn.shape), resident(g_final.shape), resident(w_in.shape),
                  resident(w_out.shape)],
        out_specs=tok,
        out_shape=jax.ShapeDtypeStruct((B, S, D), F32),
        compiler_params=pltpu.CompilerParams(dimension_semantics=("parallel", "parallel"),
                                             vmem_limit_bytes=VMEM_LIMIT),
        name="ffn",
    )(h1, ada8, g_ffn, g_final, w_in, w_out)


def _pack_weights(w_in, w_uq, w_ukv):
    D = w_in.shape[0]
    o_gate = 4 * M_WIDTH
    o_qlat = o_gate + 2 * M_HEADS
    o_kr = o_qlat + A_Q_RANK + A_KV_RANK
    o_gm = o_kr + A_ROPE
    zeros = lambda n: jnp.zeros((D, n), w_in.dtype)
    half = A_ROPE // 2
    misc = jnp.concatenate([zeros(A_NOPE), w_in[:, o_kr:o_gm], w_in[:, o_kr:o_kr + half],
                            zeros(LANES - A_QK - half)], axis=1)
    w_b = jnp.concatenate([w_in[:, :o_gate], w_in[:, o_qlat:o_kr], misc], axis=1).astype(BF16)
    w_gt = w_in[:, o_gate:o_qlat].T.astype(BF16)
    w_g = w_in[:, o_gm:].astype(BF16)
    wuq = w_uq.reshape(A_Q_RANK, A_HEADS, A_QK)
    wuq = jnp.concatenate([wuq, wuq[:, :, A_NOPE:A_NOPE + half],
                           jnp.zeros((A_Q_RANK, A_HEADS, LANES - A_QK - half), w_uq.dtype)], axis=2)
    wuq = wuq.reshape(A_Q_RANK, A_HEADS * LANES).astype(BF16)
    wkv = w_ukv.reshape(A_KV_RANK, A_HEADS, A_NOPE + A_V)
    wk = jnp.pad(wkv[:, :, :A_NOPE], ((0, 0), (0, 0), (0, LANES - A_NOPE)))
    wk = wk.reshape(A_KV_RANK, A_HEADS * LANES).astype(BF16)
    wv = wkv[:, :, A_NOPE:].reshape(A_KV_RANK, A_WIDTH).astype(BF16)
    return w_b, w_gt, w_g, wuq, wk, wv


def _rope_rows(groups):
    inv_freq = ROPE_THETA ** (-jnp.arange(0, A_ROPE, 2, dtype=F32) / A_ROPE)
    half = A_ROPE // 2
    pair = jnp.concatenate([inv_freq, inv_freq])
    rows = []
    for g in range(groups):
        row = jnp.zeros((LANES,), F32).at[A_ROPE * g:A_ROPE * (g + 1)].set(pair)
        rows.append(jnp.broadcast_to(row, (SUBLANES, LANES)))
    sign = jnp.concatenate([-jnp.ones((half,), F32), jnp.ones((half,), F32)])
    sgn = jnp.broadcast_to(jnp.tile(sign, LANES // A_ROPE), (SUBLANES, LANES))
    return jnp.concatenate(rows, axis=0), sgn


def _rep8(v):
    v = v.reshape(-1, v.shape[-1])
    return jnp.repeat(v, SUBLANES, axis=0)


def kernel(x, c, positions, w_ada, b_ada, g_mix, w_in, conv_w, conv_b, b_igate, b_fgate, g_mhead,
           g_q_lat, w_uq, g_kv_lat, w_ukv, w_branch_m, w_branch_a, w_out, g_ffn, w_ffn_in,
           w_ffn_out, g_final):
    B, S, D = x.shape
    assert w_ada.shape[0] == 1, "the final rmsnorm is fused into the (single) layer's FFN kernel"
    invf, sgn = _rope_rows(MIX_TM // LANES)
    pos3 = positions.reshape(B, S, 1)
    l = 0
    ada = _ada(c, w_ada[l], b_ada[l])
    ada8 = jnp.repeat(ada.reshape(B, 6, 1, D), SUBLANES, axis=2).reshape(B, 6 * SUBLANES, D)
    w_b, w_gt, w_g, wuq, wk, wv = _pack_weights(w_in[l], w_uq[l], w_ukv[l])
    b_col = jnp.concatenate([b_igate[l], b_fgate[l]]).astype(F32).reshape(2 * M_HEADS, 1)
    gmix = _rep8(g_mix[l])
    qm, km, vm, om, gr, qa, ka, va = _mix_in(
        x, ada8, pos3, gmix, w_b, w_gt, _rep8(conv_w[l]), _rep8(conv_b[l]), b_col,
        _rep8(g_q_lat[l]), _rep8(g_kv_lat[l]), wuq, wk, wv, invf, sgn)
    hm = _mlstm(qm, km, vm, om, gr, _rep8(g_mhead[l]))
    oa = _attn(qa, ka, va)
    h1 = _merge(x, ada8, hm, oa, gmix, w_g, w_branch_m[l].astype(BF16),
                w_branch_a[l].astype(BF16), w_out[l].astype(BF16))
    return _ffn(h1, ada8, _rep8(g_ffn[l]), _rep8(g_final), w_ffn_in[l].astype(BF16),
                w_ffn_out[l].astype(BF16))
```

```python
import functools

import jax
import jax.numpy as jnp
import numpy as np
from jax import lax
from jax.experimental import pallas as pl
from jax.experimental.pallas import tpu as pltpu

F32 = jnp.float32
BF16 = jnp.bfloat16

D_MODEL = 1024
M_HEADS = 4
M_HEAD_DIM = 128
M_WIDTH = M_HEADS * M_HEAD_DIM
M_CONV = 4
A_HEADS = 8
A_NOPE = 64
A_ROPE = 32
A_V = 64
A_QK = A_NOPE + A_ROPE
A_Q_RANK = 256
A_KV_RANK = 128
A_WIDTH = A_HEADS * A_V
ROPE_THETA = 10000.0
D_FF = 2816
NORM_EPS = 1e-6

LANES = 128
SUBLANES = 8
HALO = 8
VMEM_LIMIT = 56 * 1024 * 1024

MIX_TM = 512
MIX_TN = 256
M_CHUNK = 128
ATT_T = 256
ATT_HEADS = 2
LOG2E = 1.4426950408889634
FFN_TM = 512
MXU_N = 256
FFN_CHUNKS = ((0, 6 * MXU_N), (6 * MXU_N, D_FF))


def _sigmoid(x):
    return 1.0 / (1.0 + jnp.exp(-x))


def _silu(x):
    return x * _sigmoid(x)


def _log_sigmoid(x):
    return jnp.minimum(x, 0.0) - jnp.log1p(jnp.exp(-jnp.abs(x)))


def _rows(v8, n):
    return v8 if n == SUBLANES else jnp.concatenate([v8] * (n // SUBLANES), axis=0)


def _rms(x, g8):
    n, d = x.shape
    ms = jnp.broadcast_to(jnp.sum(x * x, axis=-1, keepdims=True), (n, LANES)) * (1.0 / d)
    r = lax.rsqrt(ms + NORM_EPS)
    if d > LANES:
        r = jnp.concatenate([r] * (d // LANES), axis=1)
    return x * r * _rows(g8, n)


def _ada_rows(ada_ref, k):
    return ada_ref[0, SUBLANES * k:SUBLANES * (k + 1), :]


def _bdot(a, b):
    return jnp.dot(a, b, preferred_element_type=F32)


def _dot_nt(a, b):
    return lax.dot_general(a, b, (((1,), (1,)), ((), ())), preferred_element_type=F32)


def _ada_kernel(c_ref, w_ref, b_ref, o_ref):
    c = c_ref[...]
    o_ref[...] = jnp.dot(_silu(c), w_ref[...], preferred_element_type=F32,
                         precision=lax.Precision.HIGHEST) + b_ref[...]


def _ada(c, w_ada, b_ada):
    B, D = c.shape
    N = w_ada.shape[1]
    return pl.pallas_call(
        _ada_kernel,
        grid=(N // D,),
        in_specs=[pl.BlockSpec((B, D), lambda j: (0, 0)),
                  pl.BlockSpec((D, D), lambda j: (0, j)),
                  pl.BlockSpec((1, D), lambda j: (0, j))],
        out_specs=pl.BlockSpec((B, D), lambda j: (0, j)),
        out_shape=jax.ShapeDtypeStruct((B, N), F32),
        compiler_params=pltpu.CompilerParams(dimension_semantics=("arbitrary",),
                                             vmem_limit_bytes=VMEM_LIMIT),
        name="ada",
    )(c, w_ada, b_ada.reshape(1, N))


def _rope_partner(blk):
    return pltpu.roll(blk, LANES - A_ROPE // 2, 1)


def _mix_in_kernel(x_ref, xh_ref, ada_ref, pos_ref, gmix_ref, w_ref, wgt_ref, cw_ref, cb_ref,
                   bcol_ref, gq_ref, gkv_ref, wuq_ref, wk_ref, wv_ref, invf_ref, sgn_ref,
                   qm_ref, km_ref, vm_ref, om_ref, gr_ref, qa_ref, ka_ref, va_ref, z_scr):
    i = pl.program_id(1)
    tm = x_ref.shape[1]
    sh1 = _ada_rows(ada_ref, 0)
    gain = gmix_ref[...] * (1.0 + _ada_rows(ada_ref, 1))

    def modulate(xv):
        return _rms(xv, gain) + _rows(sh1, xv.shape[0])

    u = modulate(x_ref[0]).astype(BF16)
    uh = modulate(xh_ref[0]).astype(BF16)

    lat = _bdot(u, w_ref[:, 4 * M_WIDTH:4 * M_WIDTH + 4 * LANES])
    q_lat = lat[:, 0:A_Q_RANK]
    kv_lat = lat[:, A_Q_RANK:A_Q_RANK + A_KV_RANK]
    misc = lat[:, A_Q_RANK + A_KV_RANK:]

    pos = pos_ref[0].astype(F32)
    groups = tm // LANES
    ang = None
    for gi in range(groups):
        term = (jnp.broadcast_to(pos[gi * LANES:(gi + 1) * LANES], (LANES, LANES))
                * _rows(invf_ref[SUBLANES * gi:SUBLANES * (gi + 1), :], LANES))
        ang = term if ang is None else ang + term
    cos_p = jnp.cos(ang)
    sin_p = jnp.sin(ang) * _rows(sgn_ref[...], LANES)
    lane_g = lax.broadcasted_iota(jnp.int32, (LANES, LANES), 1)
    rope_g = (lane_g >= A_NOPE) & (lane_g < A_QK)
    cos_parts, sin_parts = [], []
    for gi in range(groups):
        shift = (A_NOPE - A_ROPE * gi) % LANES
        cg = pltpu.roll(cos_p, shift, 1) if shift else cos_p
        sg = pltpu.roll(sin_p, shift, 1) if shift else sin_p
        cos_parts.append(jnp.where(rope_g, cg, 1.0))
        sin_parts.append(jnp.where(rope_g, sg, 0.0))
    cosv = jnp.concatenate(cos_parts, axis=0)
    sinv = jnp.concatenate(sin_parts, axis=0)
    lane = lax.broadcasted_iota(jnp.int32, (tm, LANES), 1)
    is_rope = (lane >= A_NOPE) & (lane < A_QK)

    cq = _rms(q_lat, gq_ref[...]).astype(BF16)
    ckv = _rms(kv_lat, gkv_ref[...]).astype(BF16)
    va_ref[0] = _bdot(ckv, wv_ref[...]).astype(BF16)
    krot = jnp.where(is_rope, misc * cosv + _rope_partner(misc) * sinv, 0.0)
    scale = A_QK ** -0.5 * LOG2E

    assert 2 * M_WIDTH // MIX_TN == A_HEADS // 2
    for cblk in range(2 * M_WIDTH // MIX_TN):
        cols = slice(cblk * MIX_TN, (cblk + 1) * MIX_TN)
        z_scr[0:HALO, cols] = jnp.where(i > 0, _bdot(uh, w_ref[:, cols]), 0.0)
        z_scr[HALO:HALO + tm, cols] = _bdot(u, w_ref[:, cols])
        y = _rows(cb_ref[:, cols], tm)
        for j in range(M_CONV):
            sft = M_CONV - 1 - j
            y = y + (z_scr[HALO - sft:HALO - sft + tm, cols]
                     * _rows(cw_ref[SUBLANES * j:SUBLANES * (j + 1), cols], tm))
        qk = _silu(y)
        if cblk < M_WIDTH // MIX_TN:
            qm_ref[0, :, cols] = (qk * (M_HEAD_DIM ** -0.5)).astype(BF16)
        else:
            km_ref[0, :, cblk * MIX_TN - M_WIDTH:(cblk + 1) * MIX_TN - M_WIDTH] = qk.astype(BF16)

        qa = _bdot(cq, wuq_ref[:, cols])
        kn = _bdot(ckv, wk_ref[:, cols])
        for hh in range(2):
            hcols = slice((2 * cblk + hh) * LANES, (2 * cblk + hh + 1) * LANES)
            blk = qa[:, hh * LANES:(hh + 1) * LANES]
            rot = blk * cosv + _rope_partner(blk) * sinv
            qa_ref[0, :, hcols] = (rot * scale).astype(BF16)
            ka_ref[0, :, hcols] = (kn[:, hh * LANES:(hh + 1) * LANES] + krot).astype(BF16)

        vo = _bdot(u, w_ref[:, 2 * M_WIDTH + cblk * MIX_TN:2 * M_WIDTH + (cblk + 1) * MIX_TN])
        if cblk < M_WIDTH // MIX_TN:
            vm_ref[0, :, cols] = vo.astype(BF16)
        else:
            om_ref[0, :, cblk * MIX_TN - M_WIDTH:(cblk + 1) * MIX_TN - M_WIDTH] = vo

    gpre = _dot_nt(wgt_ref[...], u) + bcol_ref[...]
    row = lax.broadcasted_iota(jnp.int32, gpre.shape, 0)
    gates = jnp.where(row < M_HEADS, gpre, _log_sigmoid(gpre))
    for cix in range(tm // M_CHUNK):
        gr_ref[0, 2 * M_HEADS * cix:2 * M_HEADS * (cix + 1), :] = gates[:, cix * M_CHUNK:(cix + 1) * M_CHUNK]


def _mix_in(x, ada8, pos3, g_mix, w_b, w_gt, conv_w, conv_b, b_col, g_q, g_kv, wuq, wk, wv,
            invf, sgn):
    B, S, D = x.shape
    tm = min(MIX_TM, S)
    nt = S // tm
    hb = tm // HALO
    const = lambda shape: pl.BlockSpec(shape, lambda b, i: (0,) * len(shape))
    tok = lambda w: pl.BlockSpec((1, tm, w), lambda b, i: (b, i, 0))
    out_shapes = (
        jax.ShapeDtypeStruct((B, S, M_WIDTH), BF16),
        jax.ShapeDtypeStruct((B, S, M_WIDTH), BF16),
        jax.ShapeDtypeStruct((B, S, M_WIDTH), BF16),
        jax.ShapeDtypeStruct((B, S, M_WIDTH), F32),
        jax.ShapeDtypeStruct((B, 2 * M_HEADS * (S // M_CHUNK), M_CHUNK), F32),
        jax.ShapeDtypeStruct((B, S, A_HEADS * LANES), BF16),
        jax.ShapeDtypeStruct((B, S, A_HEADS * LANES), BF16),
        jax.ShapeDtypeStruct((B, S, A_WIDTH), BF16),
    )
    return pl.pallas_call(
        _mix_in_kernel,
        grid=(B, nt),
        in_specs=[tok(D),
                  pl.BlockSpec((1, HALO, D), lambda b, i: (b, jnp.maximum(i * hb - 1, 0), 0)),
                  pl.BlockSpec((1,) + ada8.shape[1:], lambda b, i: (b, 0, 0)),
                  tok(1),
                  const(g_mix.shape), const(w_b.shape), const(w_gt.shape), const(conv_w.shape),
                  const(conv_b.shape), const(b_col.shape), const(g_q.shape), const(g_kv.shape),
                  const(wuq.shape), const(wk.shape), const(wv.shape), const(invf.shape),
                  const(sgn.shape)],
        out_specs=(tok(M_WIDTH), tok(M_WIDTH), tok(M_WIDTH), tok(M_WIDTH),
                   pl.BlockSpec((1, 2 * M_HEADS * (tm // M_CHUNK), M_CHUNK), lambda b, i: (b, i, 0)),
                   tok(A_HEADS * LANES), tok(A_HEADS * LANES), tok(A_WIDTH)),
        out_shape=out_shapes,
        scratch_shapes=[pltpu.VMEM((HALO + tm, 2 * M_WIDTH), F32)],
        compiler_params=pltpu.CompilerParams(dimension_semantics=("parallel", "arbitrary"),
                                             vmem_limit_bytes=VMEM_LIMIT),
        name="mix_in",
    )(x, x, ada8, pos3, g_mix, w_b, w_gt, conv_w, conv_b, b_col, g_q, g_kv, wuq, wk, wv, invf, sgn)


def _split3(x):
    hi = x.astype(BF16)
    r = x - hi.astype(F32)
    mid = r.astype(BF16)
    lo = (r - mid.astype(F32)).astype(BF16)
    return hi, mid, lo


def _mlstm_kernel(q_ref, k_ref, v_ref, om_ref, gr_ref, gmh_ref, uo_ref, o_ref,
                  c_scr, rows_scr, dw_scr, iw_scr, em_scr, wc_scr):
    S = q_ref.shape[1]
    L = M_CHUNK
    dh = M_HEAD_DIM
    H = M_HEADS
    nc = S // L
    R = 2 * H * nc

    g = gr_ref[0]
    prod = _bdot(jnp.concatenate(_split3(g), axis=0), uo_ref[...])
    prod = prod[:R] + prod[R:2 * R] + prod[2 * R:]
    cs = prod[:, :L]
    tot = prod[:, L:]
    li = pltpu.roll(g, H, 0)
    wl = (tot - cs) + li
    wmax = jnp.broadcast_to(jnp.max(wl, axis=-1, keepdims=True), (R, L))
    m_cur = jnp.zeros((2 * H, L), F32)
    m_prev, m_new = [], []
    for c in range(nc):
        rs = slice(2 * H * c, 2 * H * (c + 1))
        m_prev.append(m_cur)
        m_cur = jnp.maximum(tot[rs] + m_cur, wmax[rs])
        m_new.append(m_cur)
    m_prev = jnp.concatenate(m_prev, axis=0)
    m_new = jnp.concatenate(m_new, axis=0)
    rows_scr[0] = cs
    rows_scr[1] = li
    rows_scr[2] = wl
    rows_scr[3] = m_prev
    rows_scr[4] = m_new
    rows_scr[5] = jnp.exp(tot + m_prev - m_new)

    causal = (lax.broadcasted_iota(jnp.int32, (L, L), 0) >= lax.broadcasted_iota(jnp.int32, (L, L), 1))
    upper = lax.broadcasted_iota(jnp.int32, (2 * H, L), 0) < H

    def prep(c, slot):
        rs = pl.ds(pl.multiple_of(c * 2 * H, 2 * H), 2 * H)
        cs_c, li_c, wl_c = rows_scr[0, rs, :], rows_scr[1, rs, :], rows_scr[2, rs, :]
        mp_c, mn_c = rows_scr[3, rs, :], rows_scr[4, rs, :]
        z = jnp.where(upper, pltpu.roll(wl_c, H, 0), cs_c)
        cols = jnp.concatenate([z] * (L // (2 * H)), axis=0).T
        for h in range(H):
            wl_col = jnp.broadcast_to(cols[:, h:h + 1], (L, L))
            b_col = jnp.broadcast_to(cols[:, H + h:H + h + 1], (L, L))
            r = H + h
            dlog = jnp.where(causal, (b_col - cs_c[r:r + 1, :]) + li_c[r:r + 1, :], -jnp.inf)
            inter_log = b_col + mp_c[r:r + 1, :]
            m_t = jnp.maximum(inter_log, jnp.max(dlog, axis=-1, keepdims=True))
            dw_scr[slot, h] = jnp.exp(dlog - m_t)
            iw_scr[slot, h] = jnp.exp(inter_log - m_t)
            em_scr[slot, h] = jnp.exp(-m_t)
            wc_scr[slot, h] = jnp.exp(wl_col - mn_c[r:r + 1, :])

    c_scr[...] = jnp.zeros_like(c_scr)
    ones = jnp.ones((L, dh), BF16)

    def chunk(c, slot):
        ts = pl.ds(pl.multiple_of(c * L, L), L)
        for h in range(H):
            hs = slice(h * dh, (h + 1) * dh)
            q = q_ref[0, ts, hs]
            k = k_ref[0, ts, hs]
            v_aug = jnp.concatenate([v_ref[0, ts, hs], ones], axis=1)
            c_aug = c_scr[h]
            iw = iw_scr[slot, h]
            s = _dot_nt(q, k) * dw_scr[slot, h]
            qc = _bdot(q, c_aug.astype(BF16))
            sv = _bdot(s.astype(BF16), v_aug)
            num = sv[:, :dh] + iw * qc[:, :dh]
            den = sv[:, dh:] + iw * qc[:, dh:]
            hv = num / jnp.maximum(jnp.abs(den), em_scr[slot, h])
            hn = _rms(hv, gmh_ref[:, hs])
            o_ref[0, ts, hs] = (_sigmoid(om_ref[0, ts, hs]) * hn).astype(o_ref.dtype)

            decay = rows_scr[5, pl.ds(c * 2 * H + H + h, 1), :]
            kw_t = (k.astype(F32) * wc_scr[slot, h]).T.astype(BF16)
            c_scr[h] = jnp.concatenate([decay, decay], axis=1) * c_aug + _bdot(kw_t, v_aug)

    prep(0, 0)

    def pair(i, carry):
        c = 2 * i
        chunk(c, 0)
        prep(c + 1, 1)
        chunk(c + 1, 1)
        prep(jnp.minimum(c + 2, nc - 1), 0)
        return carry

    lax.fori_loop(0, nc // 2, pair, 0)


def _mlstm(qm, km, vm, om, gr, g_mhead):
    B, S, W = qm.shape
    L = M_CHUNK
    uo = np.concatenate([np.triu(np.ones((L, L), np.float32)), np.ones((L, L), np.float32)], axis=1)
    seq = lambda: pl.BlockSpec((1, S, W), lambda b: (b, 0, 0))
    rows = 2 * M_HEADS * (S // L)
    assert (S // L) % 2 == 0, "the chunk loop handles two chunks per iteration"
    per_head = lambda: pltpu.VMEM((2, M_HEADS, L, L), F32)
    return pl.pallas_call(
        _mlstm_kernel,
        grid=(B,),
        in_specs=[seq(), seq(), seq(), seq(),
                  pl.BlockSpec((1, rows, L), lambda b: (b, 0, 0)),
                  pl.BlockSpec((SUBLANES, W), lambda b: (0, 0)),
                  pl.BlockSpec((L, 2 * L), lambda b: (0, 0))],
        out_specs=seq(),
        out_shape=jax.ShapeDtypeStruct((B, S, W), BF16),
        scratch_shapes=[pltpu.VMEM((M_HEADS, M_HEAD_DIM, 2 * M_HEAD_DIM), F32),
                        pltpu.VMEM((6, rows, L), F32),
                        per_head(), per_head(), per_head(), per_head()],
        compiler_params=pltpu.CompilerParams(dimension_semantics=("parallel",),
                                             vmem_limit_bytes=VMEM_LIMIT),
        name="mlstm",
    )(qm, km, vm, om, gr, g_mhead, jnp.asarray(uo, BF16))


def _attn_kernel(q_ref, k_ref, v_ref, o_ref, m_scr, acc_scr, s_scr, p_scr, al_scr):
    S = q_ref.shape[1]
    t = min(ATT_T, S)
    nq = S // t
    nh = ATT_HEADS
    causal = (lax.broadcasted_iota(jnp.int32, (t, t), 1)
              <= lax.broadcasted_iota(jnp.int32, (t, t), 0))

    def tile(i):
        return pl.ds(pl.multiple_of(i * t, t), t)

    def head(hh):
        return slice(hh * LANES, (hh + 1) * LANES)

    def scores(blk, slot):
        qi, kt = blk
        for hh in range(nh):
            s_scr[slot, hh] = _dot_nt(q_ref[0, tile(qi), head(hh)], k_ref[0, tile(kt), head(hh)])

    def softmax(blk, slot, diag):
        qi, _ = blk
        for hh in range(nh):
            mask = (lambda x: jnp.where(causal, x, -jnp.inf)) if diag else (lambda x: x)
            row_max = jnp.max(mask(s_scr[slot, hh]), axis=-1, keepdims=True)
            if diag:
                m_new = jnp.broadcast_to(row_max, (t, LANES))
            else:
                m_old = m_scr[hh, tile(qi), :]
                m_new = jnp.maximum(m_old, row_max)
                al_scr[slot, hh] = jnp.exp2(m_old - m_new)
            m_scr[hh, tile(qi), :] = m_new
            p = jnp.exp2(mask(s_scr[slot, hh]) - jnp.concatenate([m_new] * (t // LANES), axis=1))
            p_scr[slot, hh // 2, (hh % 2) * t:(hh % 2 + 1) * t, :] = p.astype(BF16)

    ones = jnp.ones((t, LANES), BF16)

    def values(blk, slot, diag):
        qi, kt = blk
        for hh in range(nh):
            if hh % 2 == 0:
                pv = _bdot(p_scr[slot, hh // 2],
                           jnp.concatenate([v_ref[0, tile(kt), head(hh // 2)], ones], axis=1))
            part = pv[(hh % 2) * t:(hh % 2 + 1) * t]
            if diag:
                acc_scr[hh, tile(qi), :] = part
            else:
                alpha = al_scr[slot, hh]
                acc_scr[hh, tile(qi), :] = (jnp.concatenate([alpha, alpha], axis=1)
                                            * acc_scr[hh, tile(qi), :] + part)

    def next_lower(qi, kt):
        wrap = kt + 1 >= qi
        return jnp.where(wrap, qi + 1, qi), jnp.where(wrap, 0, kt + 1)

    nd = nq
    n = nq * (nq + 1) // 2
    i32 = jnp.int32
    known = {}

    def block(j):
        if j < nd:
            return i32(j), i32(j)
        if j not in known:
            known[j] = (i32(1), i32(0)) if j == nd else next_lower(*block(j - 1))
        return known[j]

    def emit(j):
        slot = j % 2
        if 2 <= j <= n + 1:
            values(block(j - 2), slot, j - 2 < nd)
        if 1 <= j <= n:
            softmax(block(j - 1), 1 - slot, j - 1 < nd)
        if j < n:
            scores(block(j), slot)

    def step(bc, bb, ba, slot, diag):
        values(bc, slot, diag)
        softmax(bb, 1 - slot, diag)
        scores(ba, slot)

    emit(0)
    emit(1)
    diag_pairs = max(0, (nd - 2) // 2)

    def diag_body(i, carry):
        j = 2 + 2 * i
        step((j - 2, j - 2), (j - 1, j - 1), (j, j), 0, True)
        step((j - 1, j - 1), (j, j), (j + 1, j + 1), 1, True)
        return carry

    lax.fori_loop(0, diag_pairs, diag_body, 0)
    j = 2 + 2 * diag_pairs
    while j <= n + 1 and (j < nd + 2 or j % 2):
        emit(j)
        j += 1
    low_pairs = max(0, (n - j) // 2)
    if low_pairs:
        def low_body(_, carry):
            bc, bb = carry
            ba = next_lower(*bb)
            step(bc, bb, ba, 0, False)
            bz = next_lower(*ba)
            step(bb, ba, bz, 1, False)
            return ba, bz

        bc, bb = lax.fori_loop(0, low_pairs, low_body, (block(j - 2), block(j - 1)))
        j += 2 * low_pairs
        known[j - 2], known[j - 1] = bc, bb
    while j <= n + 1:
        emit(j)
        j += 1

    lane = lax.broadcasted_iota(jnp.int32, (t, LANES), 1)

    def finish(qi, carry):
        for pair in range(nh // 2):
            outs = [acc_scr[hh, tile(qi), :LANES] / acc_scr[hh, tile(qi), LANES:]
                    for hh in (2 * pair, 2 * pair + 1)]
            o_ref[0, tile(qi), head(pair)] = jnp.where(lane < A_V, outs[0], outs[1]).astype(o_ref.dtype)
        return carry

    lax.fori_loop(0, nq, finish, 0)


def _attn(qa, ka, va):
    B, S, _ = qa.shape
    t = min(ATT_T, S)
    nh = ATT_HEADS
    group = lambda w: pl.BlockSpec((1, S, w), lambda b, p: (b, 0, p))
    return pl.pallas_call(
        _attn_kernel,
        grid=(B, A_HEADS // nh),
        in_specs=[group(nh * LANES), group(nh * LANES), group(nh * A_V)],
        out_specs=group(nh * A_V),
        out_shape=jax.ShapeDtypeStruct((B, S, A_WIDTH), BF16),
        scratch_shapes=[pltpu.VMEM((nh, S, LANES), F32), pltpu.VMEM((nh, S, 2 * LANES), F32),
                        pltpu.VMEM((2, nh, t, t), F32), pltpu.VMEM((2, nh // 2, 2 * t, t), BF16),
                        pltpu.VMEM((2, nh, t, LANES), F32)],
        compiler_params=pltpu.CompilerParams(dimension_semantics=("parallel", "parallel"),
                                             vmem_limit_bytes=VMEM_LIMIT),
        name="attn",
    )(qa, ka, va)


def _merge_kernel(x_ref, ada_ref, hm_ref, oa_ref, gmix_ref, wg_ref, wbm_ref, wba_ref, wo_ref, h_ref):
    x = x_ref[0]
    tm, D = x.shape
    gain = gmix_ref[...] * (1.0 + _ada_rows(ada_ref, 1))
    u = (_rms(x, gain) + _rows(_ada_rows(ada_ref, 0), tm)).astype(BF16)
    y = _sigmoid(_bdot(u, wg_ref[:, 0:D])) * _bdot(hm_ref[0], wbm_ref[...])
    y = y + _sigmoid(_bdot(u, wg_ref[:, D:2 * D])) * _bdot(oa_ref[0], wba_ref[...])
    h_ref[0] = x + _rows(_ada_rows(ada_ref, 2), tm) * _bdot(y.astype(BF16), wo_ref[...])


def _merge(x, ada8, hm, oa, g_mix, w_g, w_bm, w_ba, w_o):
    B, S, D = x.shape
    tm = min(MIX_TM, S)
    const = lambda shape: pl.BlockSpec(shape, lambda b, i: (0,) * len(shape))
    tok = lambda w: pl.BlockSpec((1, tm, w), lambda b, i: (b, i, 0))
    return pl.pallas_call(
        _merge_kernel,
        grid=(B, S // tm),
        in_specs=[tok(D), pl.BlockSpec((1,) + ada8.shape[1:], lambda b, i: (b, 0, 0)), tok(M_WIDTH), tok(A_WIDTH),
                  const(g_mix.shape), const(w_g.shape), const(w_bm.shape), const(w_ba.shape),
                  const(w_o.shape)],
        out_specs=tok(D),
        out_shape=jax.ShapeDtypeStruct((B, S, D), F32),
        compiler_params=pltpu.CompilerParams(dimension_semantics=("parallel", "parallel"),
                                             vmem_limit_bytes=VMEM_LIMIT),
        name="merge",
    )(x, ada8, hm, oa, g_mix, w_g, w_bm, w_ba, w_o)


def _ffn_kernel(h_ref, ada_ref, gffn_ref, gfin_ref, wi_ref, wo_ref, o_ref):
    h = h_ref[0]
    tm = h.shape[0]
    gain = gffn_ref[...] * (1.0 + _ada_rows(ada_ref, 4))
    u = (_rms(h, gain) + _rows(_ada_rows(ada_ref, 3), tm)).astype(BF16)
    acc = None
    for lo, hi in FFN_CHUNKS:
        a = _silu(_bdot(u, wi_ref[:, lo:hi])) * _bdot(u, wi_ref[:, D_FF + lo:D_FF + hi])
        part = _bdot(a.astype(BF16), wo_ref[lo:hi, :])
        acc = part if acc is None else acc + part
    o_ref[0] = _rms(h + _rows(_ada_rows(ada_ref, 5), tm) * acc, gfin_ref[...])


def _ffn(h1, ada8, g_ffn, g_final, w_in, w_out):
    B, S, D = h1.shape
    tm = min(FFN_TM, S)
    tok = pl.BlockSpec((1, tm, D), lambda b, i: (b, i, 0))
    resident = lambda shape: pl.BlockSpec(shape, lambda b, i: (0, 0), pipeline_mode=pl.Buffered(1))
    return pl.pallas_call(
        _ffn_kernel,
        grid=(B, S // tm),
        in_specs=[tok,
                  pl.BlockSpec((1,) + ada8.shape[1:], lambda b, i: (b, 0, 0)),
                  resident(g_ffn.shape), resident(g_final.shape), resident(w_in.shape),
                  resident(w_out.shape)],
        out_specs=tok,
        out_shape=jax.ShapeDtypeStruct((B, S, D), F32),
        compiler_params=pltpu.CompilerParams(dimension_semantics=("parallel", "parallel"),
                                             vmem_limit_bytes=VMEM_LIMIT),
        name="ffn",
    )(h1, ada8, g_ffn, g_final, w_in, w_out)


def _pack_weights(w_in, w_uq, w_ukv):
    D = w_in.shape[0]
    o_gate = 4 * M_WIDTH
    o_qlat = o_gate + 2 * M_HEADS
    o_kr = o_qlat + A_Q_RANK + A_KV_RANK
    o_gm = o_kr + A_ROPE
    zeros = lambda n: jnp.zeros((D, n), w_in.dtype)
    half = A_ROPE // 2
    misc = jnp.concatenate([zeros(A_NOPE), w_in[:, o_kr:o_gm], w_in[:, o_kr:o_kr + half],
                            zeros(LANES - A_QK - half)], axis=1)
    w_b = jnp.concatenate([w_in[:, :o_gate], w_in[:, o_qlat:o_kr], misc], axis=1).astype(BF16)
    w_gt = w_in[:, o_gate:o_qlat].T.astype(BF16)
    w_g = w_in[:, o_gm:].astype(BF16)
    wuq = w_uq.reshape(A_Q_RANK, A_HEADS, A_QK)
    wuq = jnp.concatenate([wuq, wuq[:, :, A_NOPE:A_NOPE + half],
                           jnp.zeros((A_Q_RANK, A_HEADS, LANES - A_QK - half), w_uq.dtype)], axis=2)
    wuq = wuq.reshape(A_Q_RANK, A_HEADS * LANES).astype(BF16)
    wkv = w_ukv.reshape(A_KV_RANK, A_HEADS, A_NOPE + A_V)
    wk = jnp.pad(wkv[:, :, :A_NOPE], ((0, 0), (0, 0), (0, LANES - A_NOPE)))
    wk = wk.reshape(A_KV_RANK, A_HEADS * LANES).astype(BF16)
    wv = wkv[:, :, A_NOPE:].reshape(A_KV_RANK, A_WIDTH).astype(BF16)
    return w_b, w_gt, w_g, wuq, wk, wv


def _rope_rows(groups):
    inv_freq = ROPE_THETA ** (-jnp.arange(0, A_ROPE, 2, dtype=F32) / A_ROPE)
    half = A_ROPE // 2
    pair = jnp.concatenate([inv_freq, inv_freq])
    rows = []
    for g in range(groups):
        row = jnp.zeros((LANES,), F32).at[A_ROPE * g:A_ROPE * (g + 1)].set(pair)
        rows.append(jnp.broadcast_to(row, (SUBLANES, LANES)))
    sign = jnp.concatenate([-jnp.ones((half,), F32), jnp.ones((half,), F32)])
    sgn = jnp.broadcast_to(jnp.tile(sign, LANES // A_ROPE), (SUBLANES, LANES))
    return jnp.concatenate(rows, axis=0), sgn


def _rep8(v):
    v = v.reshape(-1, v.shape[-1])
    return jnp.repeat(v, SUBLANES, axis=0)


def kernel(x, c, positions, w_ada, b_ada, g_mix, w_in, conv_w, conv_b, b_igate, b_fgate, g_mhead,
           g_q_lat, w_uq, g_kv_lat, w_ukv, w_branch_m, w_branch_a, w_out, g_ffn, w_ffn_in,
           w_ffn_out, g_final):
    B, S, D = x.shape
    assert w_ada.shape[0] == 1, "the final rmsnorm is fused into the (single) layer's FFN kernel"
    invf, sgn = _rope_rows(MIX_TM // LANES)
    pos3 = positions.reshape(B, S, 1)
    l = 0
    ada = _ada(c, w_ada[l], b_ada[l])
    ada8 = jnp.repeat(ada.reshape(B, 6, 1, D), SUBLANES, axis=2).reshape(B, 6 * SUBLANES, D)
    w_b, w_gt, w_g, wuq, wk, wv = _pack_weights(w_in[l], w_uq[l], w_ukv[l])
    b_col = jnp.concatenate([b_igate[l], b_fgate[l]]).astype(F32).reshape(2 * M_HEADS, 1)
    gmix = _rep8(g_mix[l])
    qm, km, vm, om, gr, qa, ka, va = _mix_in(
        x, ada8, pos3, gmix, w_b, w_gt, _rep8(conv_w[l]), _rep8(conv_b[l]), b_col,
        _rep8(g_q_lat[l]), _rep8(g_kv_lat[l]), wuq, wk, wv, invf, sgn)
    hm = _mlstm(qm, km, vm, om, gr, _rep8(g_mhead[l]))
    oa = _attn(qa, ka, va)
    h1 = _merge(x, ada8, hm, oa, gmix, w_g, w_branch_m[l].astype(BF16),
                w_branch_a[l].astype(BF16), w_out[l].astype(BF16))
    return _ffn(h1, ada8, _rep8(g_ffn[l]), _rep8(g_final), w_ffn_in[l].astype(BF16),
                w_ffn_out[l].astype(BF16))
```

```python
import functools

import jax
import jax.numpy as jnp
import numpy as np
from jax import lax
from jax.experimental import pallas as pl
from jax.experimental.pallas import tpu as pltpu

F32 = jnp.float32
BF16 = jnp.bfloat16

D_MODEL = 1024
M_HEADS = 4
M_HEAD_DIM = 128
M_WIDTH = M_HEADS * M_HEAD_DIM
M_CONV = 4
A_HEADS = 8
A_NOPE = 64
A_ROPE = 32
A_V = 64
A_QK = A_NOPE + A_ROPE
A_Q_RANK = 256
A_KV_RANK = 128
A_WIDTH = A_HEADS * A_V
ROPE_THETA = 10000.0
D_FF = 2816
NORM_EPS = 1e-6

LANES = 128
SUBLANES = 8
HALO = 8
VMEM_LIMIT = 56 * 1024 * 1024

MIX_TM = 512
MIX_TN = 256
M_CHUNK = 128
M_UNROLL = 2
ATT_T = 256
ATT_HEADS = 2
ATT_UNROLL = 12
LOG2E = 1.4426950408889634
FFN_TM = 512
MXU_N = 256
FFN_CHUNKS = ((0, 6 * MXU_N), (6 * MXU_N, D_FF))


def _sigmoid(x):
    return 1.0 / (1.0 + jnp.exp(-x))


def _silu(x):
    return x * _sigmoid(x)


def _log_sigmoid(x):
    return jnp.minimum(x, 0.0) - jnp.log1p(jnp.exp(-jnp.abs(x)))


def _rows(v8, n):
    return v8 if n == SUBLANES else jnp.concatenate([v8] * (n // SUBLANES), axis=0)


def _rms(x, g8):
    n, d = x.shape
    ms = jnp.broadcast_to(jnp.sum(x * x, axis=-1, keepdims=True), (n, LANES)) * (1.0 / d)
    r = lax.rsqrt(ms + NORM_EPS)
    if d > LANES:
        r = jnp.concatenate([r] * (d // LANES), axis=1)
    return x * r * _rows(g8, n)


def _ada_rows(ada_ref, k):
    return ada_ref[0, SUBLANES * k:SUBLANES * (k + 1), :]


def _bdot(a, b):
    return jnp.dot(a, b, preferred_element_type=F32)


def _dot_nt(a, b):
    return lax.dot_general(a, b, (((1,), (1,)), ((), ())), preferred_element_type=F32)


def _ada_kernel(c_ref, w_ref, b_ref, o_ref):
    c = c_ref[...]
    o_ref[...] = jnp.dot(_silu(c), w_ref[...], preferred_element_type=F32,
                         precision=lax.Precision.HIGHEST) + b_ref[...]


def _ada(c, w_ada, b_ada):
    B, D = c.shape
    N = w_ada.shape[1]
    return pl.pallas_call(
        _ada_kernel,
        grid=(N // D,),
        in_specs=[pl.BlockSpec((B, D), lambda j: (0, 0)),
                  pl.BlockSpec((D, D), lambda j: (0, j)),
                  pl.BlockSpec((1, D), lambda j: (0, j))],
        out_specs=pl.BlockSpec((B, D), lambda j: (0, j)),
        out_shape=jax.ShapeDtypeStruct((B, N), F32),
        compiler_params=pltpu.CompilerParams(dimension_semantics=("arbitrary",),
                                             vmem_limit_bytes=VMEM_LIMIT),
        name="ada",
    )(c, w_ada, b_ada.reshape(1, N))


def _rope_partner(blk):
    return pltpu.roll(blk, LANES - A_ROPE // 2, 1)


def _mix_in_kernel(x_ref, xh_ref, ada_ref, pos_ref, gmix_ref, w_ref, wgt_ref, cw_ref, cb_ref,
                   bcol_ref, gq_ref, gkv_ref, wuq_ref, wk_ref, wv_ref, invf_ref, sgn_ref,
                   qm_ref, km_ref, vm_ref, om_ref, gr_ref, qa_ref, ka_ref, va_ref, z_scr):
    i = pl.program_id(1)
    tm = x_ref.shape[1]
    sh1 = _ada_rows(ada_ref, 0)
    gain = gmix_ref[...] * (1.0 + _ada_rows(ada_ref, 1))

    def modulate(xv):
        return _rms(xv, gain) + _rows(sh1, xv.shape[0])

    u = modulate(x_ref[0]).astype(BF16)
    uh = modulate(xh_ref[0]).astype(BF16)

    lat = _bdot(u, w_ref[:, 4 * M_WIDTH:4 * M_WIDTH + 4 * LANES])
    q_lat = lat[:, 0:A_Q_RANK]
    kv_lat = lat[:, A_Q_RANK:A_Q_RANK + A_KV_RANK]
    misc = lat[:, A_Q_RANK + A_KV_RANK:]

    pos = pos_ref[0].astype(F32)
    groups = tm // LANES
    ang = None
    for gi in range(groups):
        term = (jnp.broadcast_to(pos[gi * LANES:(gi + 1) * LANES], (LANES, LANES))
                * _rows(invf_ref[SUBLANES * gi:SUBLANES * (gi + 1), :], LANES))
        ang = term if ang is None else ang + term
    cos_p = jnp.cos(ang)
    sin_p = jnp.sin(ang) * _rows(sgn_ref[...], LANES)
    lane_g = lax.broadcasted_iota(jnp.int32, (LANES, LANES), 1)
    rope_g = (lane_g >= A_NOPE) & (lane_g < A_QK)
    cos_parts, sin_parts = [], []
    for gi in range(groups):
        shift = (A_NOPE - A_ROPE * gi) % LANES
        cg = pltpu.roll(cos_p, shift, 1) if shift else cos_p
        sg = pltpu.roll(sin_p, shift, 1) if shift else sin_p
        cos_parts.append(jnp.where(rope_g, cg, 1.0))
        sin_parts.append(jnp.where(rope_g, sg, 0.0))
    cosv = jnp.concatenate(cos_parts, axis=0)
    sinv = jnp.concatenate(sin_parts, axis=0)
    lane = lax.broadcasted_iota(jnp.int32, (tm, LANES), 1)
    is_rope = (lane >= A_NOPE) & (lane < A_QK)

    cq = _rms(q_lat, gq_ref[...]).astype(BF16)
    ckv = _rms(kv_lat, gkv_ref[...]).astype(BF16)
    va_ref[0] = _bdot(ckv, wv_ref[...]).astype(BF16)
    krot = jnp.where(is_rope, misc * cosv + _rope_partner(misc) * sinv, 0.0)
    scale = A_QK ** -0.5 * LOG2E

    assert 2 * M_WIDTH // MIX_TN == A_HEADS // 2
    for cblk in range(2 * M_WIDTH // MIX_TN):
        cols = slice(cblk * MIX_TN, (cblk + 1) * MIX_TN)
        z_scr[0:HALO, cols] = jnp.where(i > 0, _bdot(uh, w_ref[:, cols]), 0.0)
        z_scr[HALO:HALO + tm, cols] = _bdot(u, w_ref[:, cols])
        y = _rows(cb_ref[:, cols], tm)
        for j in range(M_CONV):
            sft = M_CONV - 1 - j
            y = y + (z_scr[HALO - sft:HALO - sft + tm, cols]
                     * _rows(cw_ref[SUBLANES * j:SUBLANES * (j + 1), cols], tm))
        qk = _silu(y)
        if cblk < M_WIDTH // MIX_TN:
            qm_ref[0, :, cols] = (qk * (M_HEAD_DIM ** -0.5)).astype(BF16)
        else:
            km_ref[0, :, cblk * MIX_TN - M_WIDTH:(cblk + 1) * MIX_TN - M_WIDTH] = qk.astype(BF16)

        qa = _bdot(cq, wuq_ref[:, cols])
        kn = _bdot(ckv, wk_ref[:, cols])
        for hh in range(2):
            hcols = slice((2 * cblk + hh) * LANES, (2 * cblk + hh + 1) * LANES)
            blk = qa[:, hh * LANES:(hh + 1) * LANES]
            rot = blk * cosv + _rope_partner(blk) * sinv
            qa_ref[0, :, hcols] = (rot * scale).astype(BF16)
            ka_ref[0, :, hcols] = (kn[:, hh * LANES:(hh + 1) * LANES] + krot).astype(BF16)

        vo = _bdot(u, w_ref[:, 2 * M_WIDTH + cblk * MIX_TN:2 * M_WIDTH + (cblk + 1) * MIX_TN])
        if cblk < M_WIDTH // MIX_TN:
            vm_ref[0, :, cols] = vo.astype(BF16)
        else:
            om_ref[0, :, cblk * MIX_TN - M_WIDTH:(cblk + 1) * MIX_TN - M_WIDTH] = vo

    gpre = _dot_nt(wgt_ref[...], u) + bcol_ref[...]
    row = lax.broadcasted_iota(jnp.int32, gpre.shape, 0)
    gates = jnp.where(row < M_HEADS, gpre, _log_sigmoid(gpre))
    for cix in range(tm // M_CHUNK):
        gr_ref[0, 2 * M_HEADS * cix:2 * M_HEADS * (cix + 1), :] = gates[:, cix * M_CHUNK:(cix + 1) * M_CHUNK]


def _mix_in(x, ada8, pos3, g_mix, w_b, w_gt, conv_w, conv_b, b_col, g_q, g_kv, wuq, wk, wv,
            invf, sgn):
    B, S, D = x.shape
    tm = min(MIX_TM, S)
    nt = S // tm
    hb = tm // HALO
    const = lambda shape: pl.BlockSpec(shape, lambda b, i: (0,) * len(shape))
    tok = lambda w: pl.BlockSpec((1, tm, w), lambda b, i: (b, i, 0))
    out_shapes = (
        jax.ShapeDtypeStruct((B, S, M_WIDTH), BF16),
        jax.ShapeDtypeStruct((B, S, M_WIDTH), BF16),
        jax.ShapeDtypeStruct((B, S, M_WIDTH), BF16),
        jax.ShapeDtypeStruct((B, S, M_WIDTH), F32),
        jax.ShapeDtypeStruct((B, 2 * M_HEADS * (S // M_CHUNK), M_CHUNK), F32),
        jax.ShapeDtypeStruct((B, S, A_HEADS * LANES), BF16),
        jax.ShapeDtypeStruct((B, S, A_HEADS * LANES), BF16),
        jax.ShapeDtypeStruct((B, S, A_WIDTH), BF16),
    )
    return pl.pallas_call(
        _mix_in_kernel,
        grid=(B, nt),
        in_specs=[tok(D),
                  pl.BlockSpec((1, HALO, D), lambda b, i: (b, jnp.maximum(i * hb - 1, 0), 0)),
                  pl.BlockSpec((1,) + ada8.shape[1:], lambda b, i: (b, 0, 0)),
                  tok(1),
                  const(g_mix.shape), const(w_b.shape), const(w_gt.shape), const(conv_w.shape),
                  const(conv_b.shape), const(b_col.shape), const(g_q.shape), const(g_kv.shape),
                  const(wuq.shape), const(wk.shape), const(wv.shape), const(invf.shape),
                  const(sgn.shape)],
        out_specs=(tok(M_WIDTH), tok(M_WIDTH), tok(M_WIDTH), tok(M_WIDTH),
                   pl.BlockSpec((1, 2 * M_HEADS * (tm // M_CHUNK), M_CHUNK), lambda b, i: (b, i, 0)),
                   tok(A_HEADS * LANES), tok(A_HEADS * LANES), tok(A_WIDTH)),
        out_shape=out_shapes,
        scratch_shapes=[pltpu.VMEM((HALO + tm, 2 * M_WIDTH), F32)],
        compiler_params=pltpu.CompilerParams(dimension_semantics=("parallel", "arbitrary"),
                                             vmem_limit_bytes=VMEM_LIMIT),
        name="mix_in",
    )(x, x, ada8, pos3, g_mix, w_b, w_gt, conv_w, conv_b, b_col, g_q, g_kv, wuq, wk, wv, invf, sgn)


def _split3(x):
    hi = x.astype(BF16)
    r = x - hi.astype(F32)
    mid = r.astype(BF16)
    lo = (r - mid.astype(F32)).astype(BF16)
    return hi, mid, lo


def _mlstm_kernel(q_ref, k_ref, v_ref, om_ref, gr_ref, gmh_ref, uo_ref, o_ref,
                  c_scr, rows_scr, dw_scr, iw_scr, em_scr, wc_scr):
    S = q_ref.shape[1]
    L = M_CHUNK
    dh = M_HEAD_DIM
    H = M_HEADS
    nc = S // L
    R = 2 * H * nc

    g = gr_ref[0]
    prod = _bdot(jnp.concatenate(_split3(g), axis=0), uo_ref[...])
    prod = prod[:R] + prod[R:2 * R] + prod[2 * R:]
    cs = prod[:, :L]
    tot = prod[:, L:]
    li = pltpu.roll(g, H, 0)
    wl = (tot - cs) + li
    wmax = jnp.broadcast_to(jnp.max(wl, axis=-1, keepdims=True), (R, L))
    m_cur = jnp.zeros((2 * H, L), F32)
    m_prev, m_new = [], []
    for c in range(nc):
        rs = slice(2 * H * c, 2 * H * (c + 1))
        m_prev.append(m_cur)
        m_cur = jnp.maximum(tot[rs] + m_cur, wmax[rs])
        m_new.append(m_cur)
    m_prev = jnp.concatenate(m_prev, axis=0)
    m_new = jnp.concatenate(m_new, axis=0)
    rows_scr[0] = cs
    rows_scr[1] = li
    rows_scr[2] = wl
    rows_scr[3] = m_prev
    rows_scr[4] = m_new
    rows_scr[5] = jnp.exp(tot + m_prev - m_new)

    causal = (lax.broadcasted_iota(jnp.int32, (L, L), 0) >= lax.broadcasted_iota(jnp.int32, (L, L), 1))
    upper = lax.broadcasted_iota(jnp.int32, (2 * H, L), 0) < H

    def prep(c, slot):
        rs = pl.ds(pl.multiple_of(c * 2 * H, 2 * H), 2 * H)
        cs_c, li_c, wl_c = rows_scr[0, rs, :], rows_scr[1, rs, :], rows_scr[2, rs, :]
        mp_c, mn_c = rows_scr[3, rs, :], rows_scr[4, rs, :]
        z = jnp.where(upper, pltpu.roll(wl_c, H, 0), cs_c)
        cols = jnp.concatenate([z] * (L // (2 * H)), axis=0).T
        for h in range(H):
            wl_col = jnp.broadcast_to(cols[:, h:h + 1], (L, L))
            b_col = jnp.broadcast_to(cols[:, H + h:H + h + 1], (L, L))
            r = H + h
            dlog = jnp.where(causal, (b_col - cs_c[r:r + 1, :]) + li_c[r:r + 1, :], -jnp.inf)
            inter_log = b_col + mp_c[r:r + 1, :]
            m_t = jnp.maximum(inter_log, jnp.max(dlog, axis=-1, keepdims=True))
            dw_scr[slot, h] = jnp.exp(dlog - m_t)
            iw_scr[slot, h] = jnp.exp(inter_log - m_t)
            em_scr[slot, h] = jnp.exp(-m_t)
            wc_scr[slot, h] = jnp.exp(wl_col - mn_c[r:r + 1, :])

    c_scr[...] = jnp.zeros_like(c_scr)
    ones = jnp.ones((L, dh), BF16)

    def chunk(c, slot):
        ts = pl.ds(pl.multiple_of(c * L, L), L)
        for h in range(H):
            hs = slice(h * dh, (h + 1) * dh)
            q = q_ref[0, ts, hs]
            k = k_ref[0, ts, hs]
            v_aug = jnp.concatenate([v_ref[0, ts, hs], ones], axis=1)
            c_aug = c_scr[h]
            iw = iw_scr[slot, h]
            s = _dot_nt(q, k) * dw_scr[slot, h]
            qc = _bdot(q, c_aug.astype(BF16))
            sv = _bdot(s.astype(BF16), v_aug)
            num = sv[:, :dh] + iw * qc[:, :dh]
            den = sv[:, dh:] + iw * qc[:, dh:]
            hv = num / jnp.maximum(jnp.abs(den), em_scr[slot, h])
            hn = _rms(hv, gmh_ref[:, hs])
            o_ref[0, ts, hs] = (_sigmoid(om_ref[0, ts, hs]) * hn).astype(o_ref.dtype)

            decay = rows_scr[5, pl.ds(c * 2 * H + H + h, 1), :]
            kw_t = (k.astype(F32) * wc_scr[slot, h]).T.astype(BF16)
            c_scr[h] = jnp.concatenate([decay, decay], axis=1) * c_aug + _bdot(kw_t, v_aug)

    unroll = M_UNROLL if nc % M_UNROLL == 0 else 2
    prep(0, 0)

    def body(i, carry):
        c = unroll * i
        for k in range(unroll):
            chunk(c + k, k % 2)
            prep(jnp.minimum(c + k + 1, nc - 1), (k + 1) % 2)
        return carry

    lax.fori_loop(0, nc // unroll, body, 0)


def _mlstm(qm, km, vm, om, gr, g_mhead):
    B, S, W = qm.shape
    L = M_CHUNK
    uo = np.concatenate([np.triu(np.ones((L, L), np.float32)), np.ones((L, L), np.float32)], axis=1)
    seq = lambda: pl.BlockSpec((1, S, W), lambda b: (b, 0, 0))
    rows = 2 * M_HEADS * (S // L)
    assert (S // L) % 2 == 0, "the chunk loop handles two chunks per iteration"
    per_head = lambda: pltpu.VMEM((2, M_HEADS, L, L), F32)
    return pl.pallas_call(
        _mlstm_kernel,
        grid=(B,),
        in_specs=[seq(), seq(), seq(), seq(),
                  pl.BlockSpec((1, rows, L), lambda b: (b, 0, 0)),
                  pl.BlockSpec((SUBLANES, W), lambda b: (0, 0)),
                  pl.BlockSpec((L, 2 * L), lambda b: (0, 0))],
        out_specs=seq(),
        out_shape=jax.ShapeDtypeStruct((B, S, W), BF16),
        scratch_shapes=[pltpu.VMEM((M_HEADS, M_HEAD_DIM, 2 * M_HEAD_DIM), F32),
                        pltpu.VMEM((6, rows, L), F32),
                        per_head(), per_head(), per_head(), per_head()],
        compiler_params=pltpu.CompilerParams(dimension_semantics=("parallel",),
                                             vmem_limit_bytes=VMEM_LIMIT),
        name="mlstm",
    )(qm, km, vm, om, gr, g_mhead, jnp.asarray(uo, BF16))


def _attn_kernel(q_ref, k_ref, v_ref, o_ref, m_scr, acc_scr, s_scr, p_scr, al_scr):
    S = q_ref.shape[1]
    t = min(ATT_T, S)
    nq = S // t
    nh = ATT_HEADS
    causal = (lax.broadcasted_iota(jnp.int32, (t, t), 1)
              <= lax.broadcasted_iota(jnp.int32, (t, t), 0))

    def tile(i):
        return pl.ds(pl.multiple_of(i * t, t), t)

    def head(hh):
        return slice(hh * LANES, (hh + 1) * LANES)

    def scores(blk, slot):
        qi, kt = blk
        for hh in range(nh):
            s_scr[slot, hh] = _dot_nt(q_ref[0, tile(qi), head(hh)], k_ref[0, tile(kt), head(hh)])

    def softmax(blk, slot, diag):
        qi, _ = blk
        for hh in range(nh):
            mask = (lambda x: jnp.where(causal, x, -jnp.inf)) if diag else (lambda x: x)
            row_max = jnp.max(mask(s_scr[slot, hh]), axis=-1, keepdims=True)
            if diag:
                m_new = jnp.broadcast_to(row_max, (t, LANES))
            else:
                m_old = m_scr[hh, tile(qi), :]
                m_new = jnp.maximum(m_old, row_max)
                al_scr[slot, hh] = jnp.exp2(m_old - m_new)
            m_scr[hh, tile(qi), :] = m_new
            p = jnp.exp2(mask(s_scr[slot, hh]) - jnp.concatenate([m_new] * (t // LANES), axis=1))
            p_scr[slot, hh // 2, (hh % 2) * t:(hh % 2 + 1) * t, :] = p.astype(BF16)

    ones = jnp.ones((t, LANES), BF16)

    def values(blk, slot, diag):
        qi, kt = blk
        for hh in range(nh):
            if hh % 2 == 0:
                pv = _bdot(p_scr[slot, hh // 2],
                           jnp.concatenate([v_ref[0, tile(kt), head(hh // 2)], ones], axis=1))
            part = pv[(hh % 2) * t:(hh % 2 + 1) * t]
            if diag:
                acc_scr[hh, tile(qi), :] = part
            else:
                alpha = al_scr[slot, hh]
                acc_scr[hh, tile(qi), :] = (jnp.concatenate([alpha, alpha], axis=1)
                                            * acc_scr[hh, tile(qi), :] + part)

    def next_lower(qi, kt):
        wrap = kt + 1 >= qi
        return jnp.where(wrap, qi + 1, qi), jnp.where(wrap, 0, kt + 1)

    nd = nq
    n = nq * (nq + 1) // 2
    i32 = jnp.int32
    known = {}

    def block(j):
        if j < nd:
            return i32(j), i32(j)
        if j not in known:
            known[j] = (i32(1), i32(0)) if j == nd else next_lower(*block(j - 1))
        return known[j]

    def emit(j):
        slot = j % 2
        if 2 <= j <= n + 1:
            values(block(j - 2), slot, j - 2 < nd)
        if 1 <= j <= n:
            softmax(block(j - 1), 1 - slot, j - 1 < nd)
        if j < n:
            scores(block(j), slot)

    def step(bc, bb, ba, slot, diag):
        values(bc, slot, diag)
        softmax(bb, 1 - slot, diag)
        scores(ba, slot)

    j = 0
    while j <= n + 1 and (j < nd + 2 or j % 2):
        emit(j)
        j += 1
    low_iters = max(0, (n - j) // ATT_UNROLL)
    if low_iters:
        def low_body(_, carry):
            bc, bb = carry
            for k in range(ATT_UNROLL):
                ba = next_lower(*bb)
                step(bc, bb, ba, k % 2, False)
                bc, bb = bb, ba
            return bc, bb

        bc, bb = lax.fori_loop(0, low_iters, low_body, (block(j - 2), block(j - 1)))
        j += ATT_UNROLL * low_iters
        known[j - 2], known[j - 1] = bc, bb
    while j <= n + 1:
        emit(j)
        j += 1

    lane = lax.broadcasted_iota(jnp.int32, (t, LANES), 1)

    def finish(qi, carry):
        for pair in range(nh // 2):
            outs = [acc_scr[hh, tile(qi), :LANES] / acc_scr[hh, tile(qi), LANES:]
                    for hh in (2 * pair, 2 * pair + 1)]
            o_ref[0, tile(qi), head(pair)] = jnp.where(lane < A_V, outs[0], outs[1]).astype(o_ref.dtype)
        return carry

    lax.fori_loop(0, nq, finish, 0)


def _attn(qa, ka, va):
    B, S, _ = qa.shape
    t = min(ATT_T, S)
    nh = ATT_HEADS
    group = lambda w: pl.BlockSpec((1, S, w), lambda b, p: (b, 0, p))
    return pl.pallas_call(
        _attn_kernel,
        grid=(B, A_HEADS // nh),
        in_specs=[group(nh * LANES), group(nh * LANES), group(nh * A_V)],
        out_specs=group(nh * A_V),
        out_shape=jax.ShapeDtypeStruct((B, S, A_WIDTH), BF16),
        scratch_shapes=[pltpu.VMEM((nh, S, LANES), F32), pltpu.VMEM((nh, S, 2 * LANES), F32),
                        pltpu.VMEM((2, nh, t, t), F32), pltpu.VMEM((2, nh // 2, 2 * t, t), BF16),
                        pltpu.VMEM((2, nh, t, LANES), F32)],
        compiler_params=pltpu.CompilerParams(dimension_semantics=("parallel", "parallel"),
                                             vmem_limit_bytes=VMEM_LIMIT),
        name="attn",
    )(qa, ka, va)


def _merge_kernel(x_ref, ada_ref, hm_ref, oa_ref, gmix_ref, wg_ref, wbm_ref, wba_ref, wo_ref, h_ref):
    x = x_ref[0]
    tm, D = x.shape
    gain = gmix_ref[...] * (1.0 + _ada_rows(ada_ref, 1))
    u = (_rms(x, gain) + _rows(_ada_rows(ada_ref, 0), tm)).astype(BF16)
    y = _sigmoid(_bdot(u, wg_ref[:, 0:D])) * _bdot(hm_ref[0], wbm_ref[...])
    y = y + _sigmoid(_bdot(u, wg_ref[:, D:2 * D])) * _bdot(oa_ref[0], wba_ref[...])
    h_ref[0] = x + _rows(_ada_rows(ada_ref, 2), tm) * _bdot(y.astype(BF16), wo_ref[...])


def _merge(x, ada8, hm, oa, g_mix, w_g, w_bm, w_ba, w_o):
    B, S, D = x.shape
    tm = min(MIX_TM, S)
    const = lambda shape: pl.BlockSpec(shape, lambda b, i: (0,) * len(shape))
    tok = lambda w: pl.BlockSpec((1, tm, w), lambda b, i: (b, i, 0))
    return pl.pallas_call(
        _merge_kernel,
        grid=(B, S // tm),
        in_specs=[tok(D), pl.BlockSpec((1,) + ada8.shape[1:], lambda b, i: (b, 0, 0)), tok(M_WIDTH), tok(A_WIDTH),
                  const(g_mix.shape), const(w_g.shape), const(w_bm.shape), const(w_ba.shape),
                  const(w_o.shape)],
        out_specs=tok(D),
        out_shape=jax.ShapeDtypeStruct((B, S, D), F32),
        compiler_params=pltpu.CompilerParams(dimension_semantics=("parallel", "parallel"),
                                             vmem_limit_bytes=VMEM_LIMIT),
        name="merge",
    )(x, ada8, hm, oa, g_mix, w_g, w_bm, w_ba, w_o)


def _ffn_kernel(h_ref, ada_ref, gffn_ref, gfin_ref, wi_ref, wo_ref, o_ref):
    h = h_ref[0]
    tm = h.shape[0]
    gain = gffn_ref[...] * (1.0 + _ada_rows(ada_ref, 4))
    u = (_rms(h, gain) + _rows(_ada_rows(ada_ref, 3), tm)).astype(BF16)
    acc = None
    for lo, hi in FFN_CHUNKS:
        a = _silu(_bdot(u, wi_ref[:, lo:hi])) * _bdot(u, wi_ref[:, D_FF + lo:D_FF + hi])
        part = _bdot(a.astype(BF16), wo_ref[lo:hi, :])
        acc = part if acc is None else acc + part
    o_ref[0] = _rms(h + _rows(_ada_rows(ada_ref, 5), tm) * acc, gfin_ref[...])


def _ffn(h1, ada8, g_ffn, g_final, w_in, w_out):
    B, S, D = h1.shape
    tm = min(FFN_TM, S)
    tok = pl.BlockSpec((1, tm, D), lambda b, i: (b, i, 0))
    resident = lambda shape: pl.BlockSpec(shape, lambda b, i: (0, 0), pipeline_mode=pl.Buffered(1))
    return pl.pallas_call(
        _ffn_kernel,
        grid=(B, S // tm),
        in_specs=[tok,
                  pl.BlockSpec((1,) + ada8.shape[1:], lambda b, i: (b, 0, 0)),
                  resident(g_ffn.shape), resident(g_final.shape), resident(w_in.shape),
                  resident(w_out.shape)],
        out_specs=tok,
        out_shape=jax.ShapeDtypeStruct((B, S, D), F32),
        compiler_params=pltpu.CompilerParams(dimension_semantics=("parallel", "parallel"),
                                             vmem_limit_bytes=VMEM_LIMIT),
        name="ffn",
    )(h1, ada8, g_ffn, g_final, w_in, w_out)


def _pack_weights(w_in, w_uq, w_ukv):
    D = w_in.shape[0]
    o_gate = 4 * M_WIDTH
    o_qlat = o_gate + 2 * M_HEADS
    o_kr = o_qlat + A_Q_RANK + A_KV_RANK
    o_gm = o_kr + A_ROPE
    zeros = lambda n: jnp.zeros((D, n), w_in.dtype)
    half = A_ROPE // 2
    misc = jnp.concatenate([zeros(A_NOPE), w_in[:, o_kr:o_gm], w_in[:, o_kr:o_kr + half],
                            zeros(LANES - A_QK - half)], axis=1)
    w_b = jnp.concatenate([w_in[:, :o_gate], w_in[:, o_qlat:o_kr], misc], axis=1).astype(BF16)
    w_gt = w_in[:, o_gate:o_qlat].T.astype(BF16)
    w_g = w_in[:, o_gm:].astype(BF16)
    wuq = w_uq.reshape(A_Q_RANK, A_HEADS, A_QK)
    wuq = jnp.concatenate([wuq, wuq[:, :, A_NOPE:A_NOPE + half],
                           jnp.zeros((A_Q_RANK, A_HEADS, LANES - A_QK - half), w_uq.dtype)], axis=2)
    wuq = wuq.reshape(A_Q_RANK, A_HEADS * LANES).astype(BF16)
    wkv = w_ukv.reshape(A_KV_RANK, A_HEADS, A_NOPE + A_V)
    wk = jnp.pad(wkv[:, :, :A_NOPE], ((0, 0), (0, 0), (0, LANES - A_NOPE)))
    wk = wk.reshape(A_KV_RANK, A_HEADS * LANES).astype(BF16)
    wv = wkv[:, :, A_NOPE:].reshape(A_KV_RANK, A_WIDTH).astype(BF16)
    return w_b, w_gt, w_g, wuq, wk, wv


def _rope_rows(groups):
    inv_freq = ROPE_THETA ** (-jnp.arange(0, A_ROPE, 2, dtype=F32) / A_ROPE)
    half = A_ROPE // 2
    pair = jnp.concatenate([inv_freq, inv_freq])
    rows = []
    for g in range(groups):
        row = jnp.zeros((LANES,), F32).at[A_ROPE * g:A_ROPE * (g + 1)].set(pair)
        rows.append(jnp.broadcast_to(row, (SUBLANES, LANES)))
    sign = jnp.concatenate([-jnp.ones((half,), F32), jnp.ones((half,), F32)])
    sgn = jnp.broadcast_to(jnp.tile(sign, LANES // A_ROPE), (SUBLANES, LANES))
    return jnp.concatenate(rows, axis=0), sgn


def _rep8(v):
    v = v.reshape(-1, v.shape[-1])
    return jnp.repeat(v, SUBLANES, axis=0)


def kernel(x, c, positions, w_ada, b_ada, g_mix, w_in, conv_w, conv_b, b_igate, b_fgate, g_mhead,
           g_q_lat, w_uq, g_kv_lat, w_ukv, w_branch_m, w_branch_a, w_out, g_ffn, w_ffn_in,
           w_ffn_out, g_final):
    B, S, D = x.shape
    assert w_ada.shape[0] == 1, "the final rmsnorm is fused into the (single) layer's FFN kernel"
    invf, sgn = _rope_rows(MIX_TM // LANES)
    pos3 = positions.reshape(B, S, 1)
    l = 0
    ada = _ada(c, w_ada[l], b_ada[l])
    ada8 = jnp.repeat(ada.reshape(B, 6, 1, D), SUBLANES, axis=2).reshape(B, 6 * SUBLANES, D)
    w_b, w_gt, w_g, wuq, wk, wv = _pack_weights(w_in[l], w_uq[l], w_ukv[l])
    b_col = jnp.concatenate([b_igate[l], b_fgate[l]]).astype(F32).reshape(2 * M_HEADS, 1)
    gmix = _rep8(g_mix[l])
    qm, km, vm, om, gr, qa, ka, va = _mix_in(
        x, ada8, pos3, gmix, w_b, w_gt, _rep8(conv_w[l]), _rep8(conv_b[l]), b_col,
        _rep8(g_q_lat[l]), _rep8(g_kv_lat[l]), wuq, wk, wv, invf, sgn)
    hm = _mlstm(qm, km, vm, om, gr, _rep8(g_mhead[l]))
    oa = _attn(qa, ka, va)
    h1 = _merge(x, ada8, hm, oa, gmix, w_g, w_branch_m[l].astype(BF16),
                w_branch_a[l].astype(BF16), w_out[l].astype(BF16))
    return _ffn(h1, ada8, _rep8(g_ffn[l]), _rep8(g_final), w_ffn_in[l].astype(BF16),
                w_ffn_out[l].astype(BF16))
```

```python
import functools

import jax
import jax.numpy as jnp
import numpy as np
from jax import lax
from jax.experimental import pallas as pl
from jax.experimental.pallas import tpu as pltpu

F32 = jnp.float32
BF16 = jnp.bfloat16

D_MODEL = 1024
M_HEADS = 4
M_HEAD_DIM = 128
M_WIDTH = M_HEADS * M_HEAD_DIM
M_CONV = 4
A_HEADS = 8
A_NOPE = 64
A_ROPE = 32
A_V = 64
A_QK = A_NOPE + A_ROPE
A_Q_RANK = 256
A_KV_RANK = 128
A_WIDTH = A_HEADS * A_V
ROPE_THETA = 10000.0
D_FF = 2816
NORM_EPS = 1e-6

LANES = 128
SUBLANES = 8
HALO = 8
VMEM_LIMIT = 56 * 1024 * 1024

MIX_TM = 512
MIX_TN = 256
M_CHUNK = 128
M_UNROLL = 2
ATT_T = 256
ATT_HEADS = 2
LOG2E = 1.4426950408889634
FFN_TM = 512
MXU_N = 256
FFN_CHUNKS = ((0, 6 * MXU_N), (6 * MXU_N, D_FF))


def _sigmoid(x):
    return 1.0 / (1.0 + jnp.exp(-x))


def _silu(x):
    return x * _sigmoid(x)


def _log_sigmoid(x):
    return jnp.minimum(x, 0.0) - jnp.log1p(jnp.exp(-jnp.abs(x)))


def _rows(v8, n):
    return v8 if n == SUBLANES else jnp.concatenate([v8] * (n // SUBLANES), axis=0)


def _rms(x, g8):
    n, d = x.shape
    ms = jnp.broadcast_to(jnp.sum(x * x, axis=-1, keepdims=True), (n, LANES)) * (1.0 / d)
    r = lax.rsqrt(ms + NORM_EPS)
    if d > LANES:
        r = jnp.concatenate([r] * (d // LANES), axis=1)
    return x * r * _rows(g8, n)


def _ada_rows(ada_ref, k):
    return ada_ref[0, SUBLANES * k:SUBLANES * (k + 1), :]


def _bdot(a, b):
    return jnp.dot(a, b, preferred_element_type=F32)


def _dot_nt(a, b):
    return lax.dot_general(a, b, (((1,), (1,)), ((), ())), preferred_element_type=F32)


def _ada_kernel(c_ref, w_ref, b_ref, o_ref):
    c = c_ref[...]
    o_ref[...] = jnp.dot(_silu(c), w_ref[...], preferred_element_type=F32,
                         precision=lax.Precision.HIGHEST) + b_ref[...]


def _ada(c, w_ada, b_ada):
    B, D = c.shape
    N = w_ada.shape[1]
    return pl.pallas_call(
        _ada_kernel,
        grid=(N // D,),
        in_specs=[pl.BlockSpec((B, D), lambda j: (0, 0)),
                  pl.BlockSpec((D, D), lambda j: (0, j)),
                  pl.BlockSpec((1, D), lambda j: (0, j))],
        out_specs=pl.BlockSpec((B, D), lambda j: (0, j)),
        out_shape=jax.ShapeDtypeStruct((B, N), F32),
        compiler_params=pltpu.CompilerParams(dimension_semantics=("arbitrary",),
                                             vmem_limit_bytes=VMEM_LIMIT),
        name="ada",
    )(c, w_ada, b_ada.reshape(1, N))


def _rope_partner(blk):
    return pltpu.roll(blk, LANES - A_ROPE // 2, 1)


def _mix_in_kernel(x_ref, xh_ref, ada_ref, pos_ref, gmix_ref, w_ref, wgt_ref, cw_ref, cb_ref,
                   bcol_ref, gq_ref, gkv_ref, wuq_ref, wk_ref, wv_ref, invf_ref, sgn_ref,
                   qm_ref, km_ref, vm_ref, om_ref, gr_ref, qa_ref, ka_ref, va_ref, z_scr):
    i = pl.program_id(1)
    tm = x_ref.shape[1]
    sh1 = _ada_rows(ada_ref, 0)
    gain = gmix_ref[...] * (1.0 + _ada_rows(ada_ref, 1))

    def modulate(xv):
        return _rms(xv, gain) + _rows(sh1, xv.shape[0])

    u = modulate(x_ref[0]).astype(BF16)
    uh = modulate(xh_ref[0]).astype(BF16)

    lat = _bdot(u, w_ref[:, 4 * M_WIDTH:4 * M_WIDTH + 4 * LANES])
    q_lat = lat[:, 0:A_Q_RANK]
    kv_lat = lat[:, A_Q_RANK:A_Q_RANK + A_KV_RANK]
    misc = lat[:, A_Q_RANK + A_KV_RANK:]

    pos = pos_ref[0].astype(F32)
    groups = tm // LANES
    ang = None
    for gi in range(groups):
        term = (jnp.broadcast_to(pos[gi * LANES:(gi + 1) * LANES], (LANES, LANES))
                * _rows(invf_ref[SUBLANES * gi:SUBLANES * (gi + 1), :], LANES))
        ang = term if ang is None else ang + term
    cos_p = jnp.cos(ang)
    sin_p = jnp.sin(ang) * _rows(sgn_ref[...], LANES)
    lane_g = lax.broadcasted_iota(jnp.int32, (LANES, LANES), 1)
    rope_g = (lane_g >= A_NOPE) & (lane_g < A_QK)
    cos_parts, sin_parts = [], []
    for gi in range(groups):
        shift = (A_NOPE - A_ROPE * gi) % LANES
        cg = pltpu.roll(cos_p, shift, 1) if shift else cos_p
        sg = pltpu.roll(sin_p, shift, 1) if shift else sin_p
        cos_parts.append(jnp.where(rope_g, cg, 1.0))
        sin_parts.append(jnp.where(rope_g, sg, 0.0))
    cosv = jnp.concatenate(cos_parts, axis=0)
    sinv = jnp.concatenate(sin_parts, axis=0)
    lane = lax.broadcasted_iota(jnp.int32, (tm, LANES), 1)
    is_rope = (lane >= A_NOPE) & (lane < A_QK)

    cq = _rms(q_lat, gq_ref[...]).astype(BF16)
    ckv = _rms(kv_lat, gkv_ref[...]).astype(BF16)
    va_ref[0] = _bdot(ckv, wv_ref[...]).astype(BF16)
    krot = jnp.where(is_rope, misc * cosv + _rope_partner(misc) * sinv, 0.0)
    scale = A_QK ** -0.5 * LOG2E

    assert 2 * M_WIDTH // MIX_TN == A_HEADS // 2
    for cblk in range(2 * M_WIDTH // MIX_TN):
        cols = slice(cblk * MIX_TN, (cblk + 1) * MIX_TN)
        z_scr[0:HALO, cols] = jnp.where(i > 0, _bdot(uh, w_ref[:, cols]), 0.0)
        z_scr[HALO:HALO + tm, cols] = _bdot(u, w_ref[:, cols])
        y = _rows(cb_ref[:, cols], tm)
        for j in range(M_CONV):
            sft = M_CONV - 1 - j
            y = y + (z_scr[HALO - sft:HALO - sft + tm, cols]
                     * _rows(cw_ref[SUBLANES * j:SUBLANES * (j + 1), cols], tm))
        qk = _silu(y)
        if cblk < M_WIDTH // MIX_TN:
            qm_ref[0, :, cols] = (qk * (M_HEAD_DIM ** -0.5)).astype(BF16)
        else:
            km_ref[0, :, cblk * MIX_TN - M_WIDTH:(cblk + 1) * MIX_TN - M_WIDTH] = qk.astype(BF16)

        qa = _bdot(cq, wuq_ref[:, cols])
        kn = _bdot(ckv, wk_ref[:, cols])
        for hh in range(2):
            hcols = slice((2 * cblk + hh) * LANES, (2 * cblk + hh + 1) * LANES)
            blk = qa[:, hh * LANES:(hh + 1) * LANES]
            rot = blk * cosv + _rope_partner(blk) * sinv
            qa_ref[0, :, hcols] = (rot * scale).astype(BF16)
            ka_ref[0, :, hcols] = (kn[:, hh * LANES:(hh + 1) * LANES] + krot).astype(BF16)

        vo = _bdot(u, w_ref[:, 2 * M_WIDTH + cblk * MIX_TN:2 * M_WIDTH + (cblk + 1) * MIX_TN])
        if cblk < M_WIDTH // MIX_TN:
            vm_ref[0, :, cols] = vo.astype(BF16)
        else:
            om_ref[0, :, cblk * MIX_TN - M_WIDTH:(cblk + 1) * MIX_TN - M_WIDTH] = vo

    gpre = _dot_nt(wgt_ref[...], u) + bcol_ref[...]
    row = lax.broadcasted_iota(jnp.int32, gpre.shape, 0)
    gates = jnp.where(row < M_HEADS, gpre, _log_sigmoid(gpre))
    for cix in range(tm // M_CHUNK):
        gr_ref[0, 2 * M_HEADS * cix:2 * M_HEADS * (cix + 1), :] = gates[:, cix * M_CHUNK:(cix + 1) * M_CHUNK]


def _mix_in(x, ada8, pos3, g_mix, w_b, w_gt, conv_w, conv_b, b_col, g_q, g_kv, wuq, wk, wv,
            invf, sgn):
    B, S, D = x.shape
    tm = min(MIX_TM, S)
    nt = S // tm
    hb = tm // HALO
    const = lambda shape: pl.BlockSpec(shape, lambda b, i: (0,) * len(shape))
    tok = lambda w: pl.BlockSpec((1, tm, w), lambda b, i: (b, i, 0))
    out_shapes = (
        jax.ShapeDtypeStruct((B, S, M_WIDTH), BF16),
        jax.ShapeDtypeStruct((B, S, M_WIDTH), BF16),
        jax.ShapeDtypeStruct((B, S, M_WIDTH), BF16),
        jax.ShapeDtypeStruct((B, S, M_WIDTH), F32),
        jax.ShapeDtypeStruct((B, 2 * M_HEADS * (S // M_CHUNK), M_CHUNK), F32),
        jax.ShapeDtypeStruct((B, S, A_HEADS * LANES), BF16),
        jax.ShapeDtypeStruct((B, S, A_HEADS * LANES), BF16),
        jax.ShapeDtypeStruct((B, S, A_WIDTH), BF16),
    )
    return pl.pallas_call(
        _mix_in_kernel,
        grid=(B, nt),
        in_specs=[tok(D),
                  pl.BlockSpec((1, HALO, D), lambda b, i: (b, jnp.maximum(i * hb - 1, 0), 0)),
                  pl.BlockSpec((1,) + ada8.shape[1:], lambda b, i: (b, 0, 0)),
                  tok(1),
                  const(g_mix.shape), const(w_b.shape), const(w_gt.shape), const(conv_w.shape),
                  const(conv_b.shape), const(b_col.shape), const(g_q.shape), const(g_kv.shape),
                  const(wuq.shape), const(wk.shape), const(wv.shape), const(invf.shape),
                  const(sgn.shape)],
        out_specs=(tok(M_WIDTH), tok(M_WIDTH), tok(M_WIDTH), tok(M_WIDTH),
                   pl.BlockSpec((1, 2 * M_HEADS * (tm // M_CHUNK), M_CHUNK), lambda b, i: (b, i, 0)),
                   tok(A_HEADS * LANES), tok(A_HEADS * LANES), tok(A_WIDTH)),
        out_shape=out_shapes,
        scratch_shapes=[pltpu.VMEM((HALO + tm, 2 * M_WIDTH), F32)],
        compiler_params=pltpu.CompilerParams(dimension_semantics=("parallel", "arbitrary"),
                                             vmem_limit_bytes=VMEM_LIMIT),
        name="mix_in",
    )(x, x, ada8, pos3, g_mix, w_b, w_gt, conv_w, conv_b, b_col, g_q, g_kv, wuq, wk, wv, invf, sgn)


def _split3(x):
    hi = x.astype(BF16)
    r = x - hi.astype(F32)
    mid = r.astype(BF16)
    lo = (r - mid.astype(F32)).astype(BF16)
    return hi, mid, lo


def _mlstm_kernel(q_ref, k_ref, v_ref, om_ref, gr_ref, gmh_ref, uo_ref, o_ref,
                  c_scr, rows_scr, dw_scr, iw_scr, em_scr, wc_scr):
    S = q_ref.shape[1]
    L = M_CHUNK
    dh = M_HEAD_DIM
    H = M_HEADS
    nc = S // L
    R = 2 * H * nc

    g = gr_ref[0]
    prod = _bdot(jnp.concatenate(_split3(g), axis=0), uo_ref[...])
    prod = prod[:R] + prod[R:2 * R] + prod[2 * R:]
    cs = prod[:, :L]
    tot = prod[:, L:]
    li = pltpu.roll(g, H, 0)
    wl = (tot - cs) + li
    wmax = jnp.broadcast_to(jnp.max(wl, axis=-1, keepdims=True), (R, L))
    m_cur = jnp.zeros((2 * H, L), F32)
    m_prev, m_new = [], []
    for c in range(nc):
        rs = slice(2 * H * c, 2 * H * (c + 1))
        m_prev.append(m_cur)
        m_cur = jnp.maximum(tot[rs] + m_cur, wmax[rs])
        m_new.append(m_cur)
    m_prev = jnp.concatenate(m_prev, axis=0)
    m_new = jnp.concatenate(m_new, axis=0)
    rows_scr[0] = cs
    rows_scr[1] = li
    rows_scr[2] = wl
    rows_scr[3] = m_prev
    rows_scr[4] = m_new
    rows_scr[5] = jnp.exp(tot + m_prev - m_new)

    causal = (lax.broadcasted_iota(jnp.int32, (L, L), 0) >= lax.broadcasted_iota(jnp.int32, (L, L), 1))
    upper = lax.broadcasted_iota(jnp.int32, (2 * H, L), 0) < H

    def prep(c, slot):
        rs = pl.ds(pl.multiple_of(c * 2 * H, 2 * H), 2 * H)
        cs_c, li_c, wl_c = rows_scr[0, rs, :], rows_scr[1, rs, :], rows_scr[2, rs, :]
        mp_c, mn_c = rows_scr[3, rs, :], rows_scr[4, rs, :]
        z = jnp.where(upper, pltpu.roll(wl_c, H, 0), cs_c)
        cols = jnp.concatenate([z] * (L // (2 * H)), axis=0).T
        for h in range(H):
            wl_col = jnp.broadcast_to(cols[:, h:h + 1], (L, L))
            b_col = jnp.broadcast_to(cols[:, H + h:H + h + 1], (L, L))
            r = H + h
            dlog = jnp.where(causal, (b_col - cs_c[r:r + 1, :]) + li_c[r:r + 1, :], -jnp.inf)
            inter_log = b_col + mp_c[r:r + 1, :]
            m_t = jnp.maximum(inter_log, jnp.max(dlog, axis=-1, keepdims=True))
            dw_scr[slot, h] = jnp.exp(dlog - m_t)
            iw_scr[slot, h] = jnp.exp(inter_log - m_t)
            em_scr[slot, h] = jnp.exp(-m_t)
            wc_scr[slot, h] = jnp.exp(wl_col - mn_c[r:r + 1, :])

    c_scr[...] = jnp.zeros_like(c_scr)
    ones = jnp.ones((L, dh), BF16)

    def chunk(c, slot):
        ts = pl.ds(pl.multiple_of(c * L, L), L)
        for h in range(H):
            hs = slice(h * dh, (h + 1) * dh)
            q = q_ref[0, ts, hs]
            k = k_ref[0, ts, hs]
            v_aug = jnp.concatenate([v_ref[0, ts, hs], ones], axis=1)
            c_aug = c_scr[h]
            iw = iw_scr[slot, h]
            s = _dot_nt(q, k) * dw_scr[slot, h]
            qc = _bdot(q, c_aug.astype(BF16))
            sv = _bdot(s.astype(BF16), v_aug)
            num = sv[:, :dh] + iw * qc[:, :dh]
            den = sv[:, dh:] + iw * qc[:, dh:]
            hv = num / jnp.maximum(jnp.abs(den), em_scr[slot, h])
            hn = _rms(hv, gmh_ref[:, hs])
            o_ref[0, ts, hs] = (_sigmoid(om_ref[0, ts, hs]) * hn).astype(o_ref.dtype)

            decay = rows_scr[5, pl.ds(c * 2 * H + H + h, 1), :]
            kw_t = (k.astype(F32) * wc_scr[slot, h]).T.astype(BF16)
            c_scr[h] = jnp.concatenate([decay, decay], axis=1) * c_aug + _bdot(kw_t, v_aug)

    unroll = M_UNROLL if nc % M_UNROLL == 0 else 2
    prep(0, 0)

    def body(i, carry):
        c = unroll * i
        for k in range(unroll):
            chunk(c + k, k % 2)
            prep(jnp.minimum(c + k + 1, nc - 1), (k + 1) % 2)
        return carry

    lax.fori_loop(0, nc // unroll, body, 0)


def _mlstm(qm, km, vm, om, gr, g_mhead):
    B, S, W = qm.shape
    L = M_CHUNK
    uo = np.concatenate([np.triu(np.ones((L, L), np.float32)), np.ones((L, L), np.float32)], axis=1)
    seq = lambda: pl.BlockSpec((1, S, W), lambda b: (b, 0, 0))
    rows = 2 * M_HEADS * (S // L)
    assert (S // L) % 2 == 0, "the chunk loop handles two chunks per iteration"
    per_head = lambda: pltpu.VMEM((2, M_HEADS, L, L), F32)
    return pl.pallas_call(
        _mlstm_kernel,
        grid=(B,),
        in_specs=[seq(), seq(), seq(), seq(),
                  pl.BlockSpec((1, rows, L), lambda b: (b, 0, 0)),
                  pl.BlockSpec((SUBLANES, W), lambda b: (0, 0)),
                  pl.BlockSpec((L, 2 * L), lambda b: (0, 0))],
        out_specs=seq(),
        out_shape=jax.ShapeDtypeStruct((B, S, W), BF16),
        scratch_shapes=[pltpu.VMEM((M_HEADS, M_HEAD_DIM, 2 * M_HEAD_DIM), F32),
                        pltpu.VMEM((6, rows, L), F32),
                        per_head(), per_head(), per_head(), per_head()],
        compiler_params=pltpu.CompilerParams(dimension_semantics=("parallel",),
                                             vmem_limit_bytes=VMEM_LIMIT),
        name="mlstm",
    )(qm, km, vm, om, gr, g_mhead, jnp.asarray(uo, BF16))


def _attn_kernel(q_ref, k_ref, v_ref, o_ref, m_scr, acc_scr, s_scr, p_scr, al_scr):
    S = q_ref.shape[1]
    t = min(ATT_T, S)
    nq = S // t
    nh = ATT_HEADS
    causal = (lax.broadcasted_iota(jnp.int32, (t, t), 1)
              <= lax.broadcasted_iota(jnp.int32, (t, t), 0))

    def tile(i):
        return slice(i * t, (i + 1) * t)

    def head(hh):
        return slice(hh * LANES, (hh + 1) * LANES)

    def scores(blk, slot):
        qi, kt = blk
        for hh in range(nh):
            s_scr[slot, hh] = _dot_nt(q_ref[0, tile(qi), head(hh)], k_ref[0, tile(kt), head(hh)])

    def softmax(blk, slot, diag):
        qi, _ = blk
        for hh in range(nh):
            mask = (lambda x: jnp.where(causal, x, -jnp.inf)) if diag else (lambda x: x)
            row_max = jnp.max(mask(s_scr[slot, hh]), axis=-1, keepdims=True)
            if diag:
                m_new = jnp.broadcast_to(row_max, (t, LANES))
            else:
                m_old = m_scr[hh, tile(qi), :]
                m_new = jnp.maximum(m_old, row_max)
                al_scr[slot, hh] = jnp.exp2(m_old - m_new)
            m_scr[hh, tile(qi), :] = m_new
            p = jnp.exp2(mask(s_scr[slot, hh]) - jnp.concatenate([m_new] * (t // LANES), axis=1))
            p_scr[slot, hh // 2, (hh % 2) * t:(hh % 2 + 1) * t, :] = p.astype(BF16)

    ones = jnp.ones((t, LANES), BF16)

    def values(blk, slot, diag):
        qi, kt = blk
        for hh in range(nh):
            if hh % 2 == 0:
                pv = _bdot(p_scr[slot, hh // 2],
                           jnp.concatenate([v_ref[0, tile(kt), head(hh // 2)], ones], axis=1))
            part = pv[(hh % 2) * t:(hh % 2 + 1) * t]
            if diag:
                acc_scr[hh, tile(qi), :] = part
            else:
                alpha = al_scr[slot, hh]
                acc_scr[hh, tile(qi), :] = (jnp.concatenate([alpha, alpha], axis=1)
                                            * acc_scr[hh, tile(qi), :] + part)

    blocks = [(j, j) for j in range(nq)] + [(qi, kt) for qi in range(1, nq) for kt in range(qi)]
    n = len(blocks)
    for j in range(n + 2):
        slot = j % 2
        if j >= 2:
            values(blocks[j - 2], slot, j - 2 < nq)
        if 1 <= j <= n:
            softmax(blocks[j - 1], 1 - slot, j - 1 < nq)
        if j < n:
            scores(blocks[j], slot)

    lane = lax.broadcasted_iota(jnp.int32, (t, LANES), 1)
    for qi in range(nq):
        for pair in range(nh // 2):
            outs = [acc_scr[hh, tile(qi), :LANES] / acc_scr[hh, tile(qi), LANES:]
                    for hh in (2 * pair, 2 * pair + 1)]
            o_ref[0, tile(qi), head(pair)] = jnp.where(lane < A_V, outs[0], outs[1]).astype(o_ref.dtype)


def _attn(qa, ka, va):
    B, S, _ = qa.shape
    t = min(ATT_T, S)
    nh = ATT_HEADS
    group = lambda w: pl.BlockSpec((1, S, w), lambda b, p: (b, 0, p))
    return pl.pallas_call(
        _attn_kernel,
        grid=(B, A_HEADS // nh),
        in_specs=[group(nh * LANES), group(nh * LANES), group(nh * A_V)],
        out_specs=group(nh * A_V),
        out_shape=jax.ShapeDtypeStruct((B, S, A_WIDTH), BF16),
        scratch_shapes=[pltpu.VMEM((nh, S, LANES), F32), pltpu.VMEM((nh, S, 2 * LANES), F32),
                        pltpu.VMEM((2, nh, t, t), F32), pltpu.VMEM((2, nh // 2, 2 * t, t), BF16),
                        pltpu.VMEM((2, nh, t, LANES), F32)],
        compiler_params=pltpu.CompilerParams(dimension_semantics=("parallel", "parallel"),
                                             vmem_limit_bytes=VMEM_LIMIT),
        name="attn",
    )(qa, ka, va)


def _merge_kernel(x_ref, ada_ref, hm_ref, oa_ref, gmix_ref, wg_ref, wbm_ref, wba_ref, wo_ref, h_ref):
    x = x_ref[0]
    tm, D = x.shape
    gain = gmix_ref[...] * (1.0 + _ada_rows(ada_ref, 1))
    u = (_rms(x, gain) + _rows(_ada_rows(ada_ref, 0), tm)).astype(BF16)
    y = _sigmoid(_bdot(u, wg_ref[:, 0:D])) * _bdot(hm_ref[0], wbm_ref[...])
    y = y + _sigmoid(_bdot(u, wg_ref[:, D:2 * D])) * _bdot(oa_ref[0], wba_ref[...])
    h_ref[0] = x + _rows(_ada_rows(ada_ref, 2), tm) * _bdot(y.astype(BF16), wo_ref[...])


def _merge(x, ada8, hm, oa, g_mix, w_g, w_bm, w_ba, w_o):
    B, S, D = x.shape
    tm = min(MIX_TM, S)
    const = lambda shape: pl.BlockSpec(shape, lambda b, i: (0,) * len(shape))
    tok = lambda w: pl.BlockSpec((1, tm, w), lambda b, i: (b, i, 0))
    return pl.pallas_call(
        _merge_kernel,
        grid=(B, S // tm),
        in_specs=[tok(D), pl.BlockSpec((1,) + ada8.shape[1:], lambda b, i: (b, 0, 0)), tok(M_WIDTH), tok(A_WIDTH),
                  const(g_mix.shape), const(w_g.shape), const(w_bm.shape), const(w_ba.shape),
                  const(w_o.shape)],
        out_specs=tok(D),
        out_shape=jax.ShapeDtypeStruct((B, S, D), F32),
        compiler_params=pltpu.CompilerParams(dimension_semantics=("parallel", "parallel"),
                                             vmem_limit_bytes=VMEM_LIMIT),
        name="merge",
    )(x, ada8, hm, oa, g_mix, w_g, w_bm, w_ba, w_o)


def _ffn_kernel(h_ref, ada_ref, gffn_ref, gfin_ref, wi_ref, wo_ref, o_ref):
    h = h_ref[0]
    tm = h.shape[0]
    gain = gffn_ref[...] * (1.0 + _ada_rows(ada_ref, 4))
    u = (_rms(h, gain) + _rows(_ada_rows(ada_ref, 3), tm)).astype(BF16)
    acc = None
    for lo, hi in FFN_CHUNKS:
        a = _silu(_bdot(u, wi_ref[:, lo:hi])) * _bdot(u, wi_ref[:, D_FF + lo:D_FF + hi])
        part = _bdot(a.astype(BF16), wo_ref[lo:hi, :])
        acc = part if acc is None else acc + part
    o_ref[0] = _rms(h + _rows(_ada_rows(ada_ref, 5), tm) * acc, gfin_ref[...])


def _ffn(h1, ada8, g_ffn, g_final, w_in, w_out):
    B, S, D = h1.shape
    tm = min(FFN_TM, S)
    tok = pl.BlockSpec((1, tm, D), lambda b, i: (b, i, 0))
    resident = lambda shape: pl.BlockSpec(shape, lambda b, i: (0, 0), pipeline_mode=pl.Buffered(1))
    return pl.pallas_call(
        _ffn_kernel,
        grid=(B, S // tm),
        in_specs=[tok,
                  pl.BlockSpec((1,) + ada8.shape[1:], lambda b, i: (b, 0, 0)),
                  resident(g_ffn.shape), resident(g_final.shape), resident(w_in.shape),
                  resident(w_out.shape)],
        out_specs=tok,
        out_shape=jax.ShapeDtypeStruct((B, S, D), F32),
        compiler_params=pltpu.CompilerParams(dimension_semantics=("parallel", "parallel"),
                                             vmem_limit_bytes=VMEM_LIMIT),
        name="ffn",
    )(h1, ada8, g_ffn, g_final, w_in, w_out)


def _pack_weights(w_in, w_uq, w_ukv):
    D = w_in.shape[0]
    o_gate = 4 * M_WIDTH
    o_qlat = o_gate + 2 * M_HEADS
    o_kr = o_qlat + A_Q_RANK + A_KV_RANK
    o_gm = o_kr + A_ROPE
    zeros = lambda n: jnp.zeros((D, n), w_in.dtype)
    half = A_ROPE // 2
    misc = jnp.concatenate([zeros(A_NOPE), w_in[:, o_kr:o_gm], w_in[:, o_kr:o_kr + half],
                            zeros(LANES - A_QK - half)], axis=1)
    w_b = jnp.concatenate([w_in[:, :o_gate], w_in[:, o_qlat:o_kr], misc], axis=1).astype(BF16)
    w_gt = w_in[:, o_gate:o_qlat].T.astype(BF16)
    w_g = w_in[:, o_gm:].astype(BF16)
    wuq = w_uq.reshape(A_Q_RANK, A_HEADS, A_QK)
    wuq = jnp.concatenate([wuq, wuq[:, :, A_NOPE:A_NOPE + half],
                           jnp.zeros((A_Q_RANK, A_HEADS, LANES - A_QK - half), w_uq.dtype)], axis=2)
    wuq = wuq.reshape(A_Q_RANK, A_HEADS * LANES).astype(BF16)
    wkv = w_ukv.reshape(A_KV_RANK, A_HEADS, A_NOPE + A_V)
    wk = jnp.pad(wkv[:, :, :A_NOPE], ((0, 0), (0, 0), (0, LANES - A_NOPE)))
    wk = wk.reshape(A_KV_RANK, A_HEADS * LANES).astype(BF16)
    wv = wkv[:, :, A_NOPE:].reshape(A_KV_RANK, A_WIDTH).astype(BF16)
    return w_b, w_gt, w_g, wuq, wk, wv


def _rope_rows(groups):
    inv_freq = ROPE_THETA ** (-jnp.arange(0, A_ROPE, 2, dtype=F32) / A_ROPE)
    half = A_ROPE // 2
    pair = jnp.concatenate([inv_freq, inv_freq])
    rows = []
    for g in range(groups):
        row = jnp.zeros((LANES,), F32).at[A_ROPE * g:A_ROPE * (g + 1)].set(pair)
        rows.append(jnp.broadcast_to(row, (SUBLANES, LANES)))
    sign = jnp.concatenate([-jnp.ones((half,), F32), jnp.ones((half,), F32)])
    sgn = jnp.broadcast_to(jnp.tile(sign, LANES // A_ROPE), (SUBLANES, LANES))
    return jnp.concatenate(rows, axis=0), sgn


def _rep8(v):
    v = v.reshape(-1, v.shape[-1])
    return jnp.repeat(v, SUBLANES, axis=0)


def kernel(x, c, positions, w_ada, b_ada, g_mix, w_in, conv_w, conv_b, b_igate, b_fgate, g_mhead,
           g_q_lat, w_uq, g_kv_lat, w_ukv, w_branch_m, w_branch_a, w_out, g_ffn, w_ffn_in,
           w_ffn_out, g_final):
    B, S, D = x.shape
    assert w_ada.shape[0] == 1, "the final rmsnorm is fused into the (single) layer's FFN kernel"
    invf, sgn = _rope_rows(MIX_TM // LANES)
    pos3 = positions.reshape(B, S, 1)
    l = 0
    ada = _ada(c, w_ada[l], b_ada[l])
    ada8 = jnp.repeat(ada.reshape(B, 6, 1, D), SUBLANES, axis=2).reshape(B, 6 * SUBLANES, D)
    w_b, w_gt, w_g, wuq, wk, wv = _pack_weights(w_in[l], w_uq[l], w_ukv[l])
    b_col = jnp.concatenate([b_igate[l], b_fgate[l]]).astype(F32).reshape(2 * M_HEADS, 1)
    gmix = _rep8(g_mix[l])
    qm, km, vm, om, gr, qa, ka, va = _mix_in(
        x, ada8, pos3, gmix, w_b, w_gt, _rep8(conv_w[l]), _rep8(conv_b[l]), b_col,
        _rep8(g_q_lat[l]), _rep8(g_kv_lat[l]), wuq, wk, wv, invf, sgn)
    hm = _mlstm(qm, km, vm, om, gr, _rep8(g_mhead[l]))
    oa = _attn(qa, ka, va)
    h1 = _merge(x, ada8, hm, oa, gmix, w_g, w_branch_m[l].astype(BF16),
                w_branch_a[l].astype(BF16), w_out[l].astype(BF16))
    return _ffn(h1, ada8, _rep8(g_ffn[l]), _rep8(g_final), w_ffn_in[l].astype(BF16),
                w_ffn_out[l].astype(BF16))
```

```python
import functools

import jax
import jax.numpy as jnp
import numpy as np
from jax import lax
from jax.experimental import pallas as pl
from jax.experimental.pallas import tpu as pltpu

F32 = jnp.float32
BF16 = jnp.bfloat16

D_MODEL = 1024
M_HEADS = 4
M_HEAD_DIM = 128
M_WIDTH = M_HEADS * M_HEAD_DIM
M_CONV = 4
A_HEADS = 8
A_NOPE = 64
A_ROPE = 32
A_V = 64
A_QK = A_NOPE + A_ROPE
A_Q_RANK = 256
A_KV_RANK = 128
A_WIDTH = A_HEADS * A_V
ROPE_THETA = 10000.0
D_FF = 2816
NORM_EPS = 1e-6

LANES = 128
SUBLANES = 8
HALO = 8
VMEM_LIMIT = 56 * 1024 * 1024

MIX_TM = 512
MERGE_TM = 1024
MIX_TN = 256
M_CHUNK = 128
M_UNROLL = 2
ATT_T = 256
ATT_HEADS = 2
LOG2E = 1.4426950408889634
FFN_TM = 1024
MXU_N = 256
FFN_CHUNKS = ((0, 6 * MXU_N), (6 * MXU_N, D_FF))


def _sigmoid(x):
    return 0.5 * jnp.tanh(0.5 * x) + 0.5


def _silu(x):
    return x * _sigmoid(x)


def _log_sigmoid(x):
    return jnp.minimum(x, 0.0) - jnp.log1p(jnp.exp(-jnp.abs(x)))


def _rows(v8, n):
    return v8 if n == SUBLANES else jnp.concatenate([v8] * (n // SUBLANES), axis=0)


def _rms(x, g8):
    n, d = x.shape
    ms = jnp.broadcast_to(jnp.sum(x * x, axis=-1, keepdims=True), (n, LANES)) * (1.0 / d)
    r = lax.rsqrt(ms + NORM_EPS)
    if d > LANES:
        r = jnp.concatenate([r] * (d // LANES), axis=1)
    return x * r * _rows(g8, n)


def _ada_rows(ada_ref, k):
    return ada_ref[0, SUBLANES * k:SUBLANES * (k + 1), :]


def _bdot(a, b):
    return jnp.dot(a, b, preferred_element_type=F32)


def _dot_nt(a, b):
    return lax.dot_general(a, b, (((1,), (1,)), ((), ())), preferred_element_type=F32)


def _ada_kernel(c_ref, w_ref, b_ref, o_ref):
    c = c_ref[...]
    o_ref[...] = jnp.dot(_silu(c), w_ref[...], preferred_element_type=F32,
                         precision=lax.Precision.HIGHEST) + b_ref[...]


def _ada(c, w_ada, b_ada):
    B, D = c.shape
    N = w_ada.shape[1]
    return pl.pallas_call(
        _ada_kernel,
        grid=(N // D,),
        in_specs=[pl.BlockSpec((B, D), lambda j: (0, 0)),
                  pl.BlockSpec((D, D), lambda j: (0, j)),
                  pl.BlockSpec((1, D), lambda j: (0, j))],
        out_specs=pl.BlockSpec((B, D), lambda j: (0, j)),
        out_shape=jax.ShapeDtypeStruct((B, N), F32),
        compiler_params=pltpu.CompilerParams(dimension_semantics=("arbitrary",),
                                             vmem_limit_bytes=VMEM_LIMIT),
        name="ada",
    )(c, w_ada, b_ada.reshape(1, N))


def _rope_partner(blk):
    return pltpu.roll(blk, LANES - A_ROPE // 2, 1)


def _mix_in_kernel(x_ref, xh_ref, ada_ref, pos_ref, gmix_ref, w_ref, wgt_ref, cw_ref, cb_ref,
                   bcol_ref, gq_ref, gkv_ref, wuq_ref, wk_ref, wv_ref, invf_ref, sgn_ref,
                   qm_ref, km_ref, vm_ref, om_ref, gr_ref, qa_ref, ka_ref, va_ref, z_scr):
    i = pl.program_id(1)
    tm = x_ref.shape[1]
    sh1 = _ada_rows(ada_ref, 0)
    gain = gmix_ref[...] * (1.0 + _ada_rows(ada_ref, 1))

    def modulate(xv):
        return _rms(xv, gain) + _rows(sh1, xv.shape[0])

    u = modulate(x_ref[0]).astype(BF16)
    uh = modulate(xh_ref[0]).astype(BF16)

    lat = _bdot(u, w_ref[:, 4 * M_WIDTH:4 * M_WIDTH + 4 * LANES])
    q_lat = lat[:, 0:A_Q_RANK]
    kv_lat = lat[:, A_Q_RANK:A_Q_RANK + A_KV_RANK]
    misc = lat[:, A_Q_RANK + A_KV_RANK:]

    pos = pos_ref[0].astype(F32)
    groups = tm // LANES
    ang = None
    for gi in range(groups):
        term = (jnp.broadcast_to(pos[gi * LANES:(gi + 1) * LANES], (LANES, LANES))
                * _rows(invf_ref[SUBLANES * gi:SUBLANES * (gi + 1), :], LANES))
        ang = term if ang is None else ang + term
    cos_p = jnp.cos(ang)
    sin_p = jnp.sin(ang) * _rows(sgn_ref[...], LANES)
    lane_g = lax.broadcasted_iota(jnp.int32, (LANES, LANES), 1)
    rope_g = (lane_g >= A_NOPE) & (lane_g < A_QK)
    cos_parts, sin_parts = [], []
    for gi in range(groups):
        shift = (A_NOPE - A_ROPE * gi) % LANES
        cg = pltpu.roll(cos_p, shift, 1) if shift else cos_p
        sg = pltpu.roll(sin_p, shift, 1) if shift else sin_p
        cos_parts.append(jnp.where(rope_g, cg, 1.0))
        sin_parts.append(jnp.where(rope_g, sg, 0.0))
    cosv = jnp.concatenate(cos_parts, axis=0)
    sinv = jnp.concatenate(sin_parts, axis=0)
    lane = lax.broadcasted_iota(jnp.int32, (tm, LANES), 1)
    is_rope = (lane >= A_NOPE) & (lane < A_QK)

    cq = _rms(q_lat, gq_ref[...]).astype(BF16)
    ckv = _rms(kv_lat, gkv_ref[...]).astype(BF16)
    va_ref[0] = _bdot(ckv, wv_ref[...]).astype(BF16)
    krot = jnp.where(is_rope, misc * cosv + _rope_partner(misc) * sinv, 0.0)
    scale = A_QK ** -0.5 * LOG2E

    assert 2 * M_WIDTH // MIX_TN == A_HEADS // 2
    for cblk in range(2 * M_WIDTH // MIX_TN):
        cols = slice(cblk * MIX_TN, (cblk + 1) * MIX_TN)
        z_scr[0:HALO, cols] = jnp.where(i > 0, _bdot(uh, w_ref[:, cols]), 0.0)
        z_scr[HALO:HALO + tm, cols] = _bdot(u, w_ref[:, cols])
        y = _rows(cb_ref[:, cols], tm)
        for j in range(M_CONV):
            sft = M_CONV - 1 - j
            y = y + (z_scr[HALO - sft:HALO - sft + tm, cols]
                     * _rows(cw_ref[SUBLANES * j:SUBLANES * (j + 1), cols], tm))
        qk = _silu(y)
        if cblk < M_WIDTH // MIX_TN:
            qm_ref[0, :, cols] = (qk * (M_HEAD_DIM ** -0.5)).astype(BF16)
        else:
            km_ref[0, :, cblk * MIX_TN - M_WIDTH:(cblk + 1) * MIX_TN - M_WIDTH] = qk.astype(BF16)

        qa = _bdot(cq, wuq_ref[:, cols])
        kn = _bdot(ckv, wk_ref[:, cols])
        for hh in range(2):
            hcols = slice((2 * cblk + hh) * LANES, (2 * cblk + hh + 1) * LANES)
            blk = qa[:, hh * LANES:(hh + 1) * LANES]
            rot = blk * cosv + _rope_partner(blk) * sinv
            qa_ref[0, :, hcols] = (rot * scale).astype(BF16)
            ka_ref[0, :, hcols] = (kn[:, hh * LANES:(hh + 1) * LANES] + krot).astype(BF16)

        vo = _bdot(u, w_ref[:, 2 * M_WIDTH + cblk * MIX_TN:2 * M_WIDTH + (cblk + 1) * MIX_TN])
        if cblk < M_WIDTH // MIX_TN:
            vm_ref[0, :, cols] = vo.astype(BF16)
        else:
            om_ref[0, :, cblk * MIX_TN - M_WIDTH:(cblk + 1) * MIX_TN - M_WIDTH] = vo

    gpre = _dot_nt(wgt_ref[...], u) + bcol_ref[...]
    row = lax.broadcasted_iota(jnp.int32, gpre.shape, 0)
    gates = jnp.where(row < M_HEADS, gpre, _log_sigmoid(gpre))
    for cix in range(tm // M_CHUNK):
        gr_ref[0, 2 * M_HEADS * cix:2 * M_HEADS * (cix + 1), :] = gates[:, cix * M_CHUNK:(cix + 1) * M_CHUNK]


def _mix_in(x, ada8, pos3, g_mix, w_b, w_gt, conv_w, conv_b, b_col, g_q, g_kv, wuq, wk, wv,
            invf, sgn):
    B, S, D = x.shape
    tm = min(MIX_TM, S)
    nt = S // tm
    hb = tm // HALO
    const = lambda shape: pl.BlockSpec(shape, lambda b, i: (0,) * len(shape))
    tok = lambda w: pl.BlockSpec((1, tm, w), lambda b, i: (b, i, 0))
    out_shapes = (
        jax.ShapeDtypeStruct((B, S, M_WIDTH), BF16),
        jax.ShapeDtypeStruct((B, S, M_WIDTH), BF16),
        jax.ShapeDtypeStruct((B, S, M_WIDTH), BF16),
        jax.ShapeDtypeStruct((B, S, M_WIDTH), F32),
        jax.ShapeDtypeStruct((B, 2 * M_HEADS * (S // M_CHUNK), M_CHUNK), F32),
        jax.ShapeDtypeStruct((B, S, A_HEADS * LANES), BF16),
        jax.ShapeDtypeStruct((B, S, A_HEADS * LANES), BF16),
        jax.ShapeDtypeStruct((B, S, A_WIDTH), BF16),
    )
    return pl.pallas_call(
        _mix_in_kernel,
        grid=(B, nt),
        in_specs=[tok(D),
                  pl.BlockSpec((1, HALO, D), lambda b, i: (b, jnp.maximum(i * hb - 1, 0), 0)),
                  pl.BlockSpec((1,) + ada8.shape[1:], lambda b, i: (b, 0, 0)),
                  tok(1),
                  const(g_mix.shape), const(w_b.shape), const(w_gt.shape), const(conv_w.shape),
                  const(conv_b.shape), const(b_col.shape), const(g_q.shape), const(g_kv.shape),
                  const(wuq.shape), const(wk.shape), const(wv.shape), const(invf.shape),
                  const(sgn.shape)],
        out_specs=(tok(M_WIDTH), tok(M_WIDTH), tok(M_WIDTH), tok(M_WIDTH),
                   pl.BlockSpec((1, 2 * M_HEADS * (tm // M_CHUNK), M_CHUNK), lambda b, i: (b, i, 0)),
                   tok(A_HEADS * LANES), tok(A_HEADS * LANES), tok(A_WIDTH)),
        out_shape=out_shapes,
        scratch_shapes=[pltpu.VMEM((HALO + tm, 2 * M_WIDTH), F32)],
        compiler_params=pltpu.CompilerParams(dimension_semantics=("parallel", "arbitrary"),
                                             vmem_limit_bytes=VMEM_LIMIT),
        name="mix_in",
    )(x, x, ada8, pos3, g_mix, w_b, w_gt, conv_w, conv_b, b_col, g_q, g_kv, wuq, wk, wv, invf, sgn)


def _split3(x):
    hi = x.astype(BF16)
    r = x - hi.astype(F32)
    mid = r.astype(BF16)
    lo = (r - mid.astype(F32)).astype(BF16)
    return hi, mid, lo


def _mlstm_kernel(q_ref, k_ref, v_ref, om_ref, gr_ref, gmh_ref, uo_ref, o_ref,
                  c_scr, rows_scr, dw_scr, iw_scr, em_scr, wc_scr):
    S = q_ref.shape[1]
    L = M_CHUNK
    dh = M_HEAD_DIM
    H = M_HEADS
    nc = S // L
    R = 2 * H * nc

    g = gr_ref[0]
    prod = _bdot(jnp.concatenate(_split3(g), axis=0), uo_ref[...])
    prod = prod[:R] + prod[R:2 * R] + prod[2 * R:]
    cs = prod[:, :L]
    tot = prod[:, L:]
    li = pltpu.roll(g, H, 0)
    wl = (tot - cs) + li
    wmax = jnp.broadcast_to(jnp.max(wl, axis=-1, keepdims=True), (R, L))
    m_cur = jnp.zeros((2 * H, L), F32)
    m_prev, m_new = [], []
    for c in range(nc):
        rs = slice(2 * H * c, 2 * H * (c + 1))
        m_prev.append(m_cur)
        m_cur = jnp.maximum(tot[rs] + m_cur, wmax[rs])
        m_new.append(m_cur)
    m_prev = jnp.concatenate(m_prev, axis=0)
    m_new = jnp.concatenate(m_new, axis=0)
    rows_scr[0] = cs
    rows_scr[1] = li
    rows_scr[2] = wl
    rows_scr[3] = m_prev
    rows_scr[4] = m_new
    rows_scr[5] = jnp.exp(tot + m_prev - m_new)

    causal = (lax.broadcasted_iota(jnp.int32, (L, L), 0) >= lax.broadcasted_iota(jnp.int32, (L, L), 1))
    upper = lax.broadcasted_iota(jnp.int32, (2 * H, L), 0) < H

    def prep(c, slot):
        rs = pl.ds(pl.multiple_of(c * 2 * H, 2 * H), 2 * H)
        cs_c, li_c, wl_c = rows_scr[0, rs, :], rows_scr[1, rs, :], rows_scr[2, rs, :]
        mp_c, mn_c = rows_scr[3, rs, :], rows_scr[4, rs, :]
        z = jnp.where(upper, pltpu.roll(wl_c, H, 0), cs_c)
        cols = jnp.concatenate([z] * (L // (2 * H)), axis=0).T
        for h in range(H):
            wl_col = jnp.broadcast_to(cols[:, h:h + 1], (L, L))
            b_col = jnp.broadcast_to(cols[:, H + h:H + h + 1], (L, L))
            r = H + h
            dlog = jnp.where(causal, (b_col - cs_c[r:r + 1, :]) + li_c[r:r + 1, :], -jnp.inf)
            inter_log = b_col + mp_c[r:r + 1, :]
            m_t = jnp.maximum(inter_log, jnp.max(dlog, axis=-1, keepdims=True))
            dw_scr[slot, h] = jnp.exp(dlog - m_t)
            iw_scr[slot, h] = jnp.exp(inter_log - m_t)
            em_scr[slot, h] = jnp.exp(-m_t)
            wc_scr[slot, h] = jnp.exp(wl_col - mn_c[r:r + 1, :])

    c_scr[...] = jnp.zeros_like(c_scr)
    ones = jnp.ones((L, dh), BF16)

    def chunk(c, slot):
        ts = pl.ds(pl.multiple_of(c * L, L), L)
        for h in range(H):
            hs = slice(h * dh, (h + 1) * dh)
            q = q_ref[0, ts, hs]
            k = k_ref[0, ts, hs]
            v_aug = jnp.concatenate([v_ref[0, ts, hs], ones], axis=1)
            c_aug = c_scr[h]
            iw = iw_scr[slot, h]
            s = _dot_nt(q, k) * dw_scr[slot, h]
            qc = _bdot(q, c_aug.astype(BF16))
            sv = _bdot(s.astype(BF16), v_aug)
            num = sv[:, :dh] + iw * qc[:, :dh]
            den = sv[:, dh:] + iw * qc[:, dh:]
            hv = num / jnp.maximum(jnp.abs(den), em_scr[slot, h])
            hn = _rms(hv, gmh_ref[:, hs])
            o_ref[0, ts, hs] = (_sigmoid(om_ref[0, ts, hs]) * hn).astype(o_ref.dtype)

            decay = rows_scr[5, pl.ds(c * 2 * H + H + h, 1), :]
            kw_t = (k.astype(F32) * wc_scr[slot, h]).T.astype(BF16)
            c_scr[h] = jnp.concatenate([decay, decay], axis=1) * c_aug + _bdot(kw_t, v_aug)

    unroll = M_UNROLL if nc % M_UNROLL == 0 else 2
    prep(0, 0)

    def body(i, carry):
        c = unroll * i
        for k in range(unroll):
            chunk(c + k, k % 2)
            prep(jnp.minimum(c + k + 1, nc - 1), (k + 1) % 2)
        return carry

    lax.fori_loop(0, nc // unroll, body, 0)


def _mlstm(qm, km, vm, om, gr, g_mhead):
    B, S, W = qm.shape
    L = M_CHUNK
    uo = np.concatenate([np.triu(np.ones((L, L), np.float32)), np.ones((L, L), np.float32)], axis=1)
    seq = lambda: pl.BlockSpec((1, S, W), lambda b: (b, 0, 0))
    rows = 2 * M_HEADS * (S // L)
    assert (S // L) % 2 == 0, "the chunk loop handles two chunks per iteration"
    per_head = lambda: pltpu.VMEM((2, M_HEADS, L, L), F32)
    return pl.pallas_call(
        _mlstm_kernel,
        grid=(B,),
        in_specs=[seq(), seq(), seq(), seq(),
                  pl.BlockSpec((1, rows, L), lambda b: (b, 0, 0)),
                  pl.BlockSpec((SUBLANES, W), lambda b: (0, 0)),
                  pl.BlockSpec((L, 2 * L), lambda b: (0, 0))],
        out_specs=seq(),
        out_shape=jax.ShapeDtypeStruct((B, S, W), BF16),
        scratch_shapes=[pltpu.VMEM((M_HEADS, M_HEAD_DIM, 2 * M_HEAD_DIM), F32),
                        pltpu.VMEM((6, rows, L), F32),
                        per_head(), per_head(), per_head(), per_head()],
        compiler_params=pltpu.CompilerParams(dimension_semantics=("parallel",),
                                             vmem_limit_bytes=VMEM_LIMIT),
        name="mlstm",
    )(qm, km, vm, om, gr, g_mhead, jnp.asarray(uo, BF16))


def _attn_kernel(q_ref, k_ref, v_ref, o_ref, m_scr, acc_scr, s_scr, p_scr, al_scr):
    S = q_ref.shape[1]
    t = min(ATT_T, S)
    nq = S // t
    nh = ATT_HEADS
    causal = (lax.broadcasted_iota(jnp.int32, (t, t), 1)
              <= lax.broadcasted_iota(jnp.int32, (t, t), 0))

    def tile(i):
        return slice(i * t, (i + 1) * t)

    def head(hh):
        return slice(hh * LANES, (hh + 1) * LANES)

    def scores(blk, slot):
        qi, kt = blk
        for hh in range(nh):
            s_scr[slot, hh] = _dot_nt(q_ref[0, tile(qi), head(hh)], k_ref[0, tile(kt), head(hh)])

    def softmax(blk, slot, diag):
        qi, _ = blk
        for hh in range(nh):
            mask = (lambda x: jnp.where(causal, x, -jnp.inf)) if diag else (lambda x: x)
            row_max = jnp.max(mask(s_scr[slot, hh]), axis=-1, keepdims=True)
            if diag:
                m_new = jnp.broadcast_to(row_max, (t, LANES))
            else:
                m_old = m_scr[hh, tile(qi), :]
                m_new = jnp.maximum(m_old, row_max)
                al_scr[slot, hh] = jnp.exp2(m_old - m_new)
            m_scr[hh, tile(qi), :] = m_new
            p = jnp.exp2(mask(s_scr[slot, hh]) - jnp.concatenate([m_new] * (t // LANES), axis=1))
            p_scr[slot, hh // 2, (hh % 2) * t:(hh % 2 + 1) * t, :] = p.astype(BF16)

    ones = jnp.ones((t, LANES), BF16)

    def values(blk, slot, diag):
        qi, kt = blk
        for hh in range(nh):
            if hh % 2 == 0:
                pv = _bdot(p_scr[slot, hh // 2],
                           jnp.concatenate([v_ref[0, tile(kt), head(hh // 2)], ones], axis=1))
            part = pv[(hh % 2) * t:(hh % 2 + 1) * t]
            if diag:
                acc_scr[hh, tile(qi), :] = part
            else:
                alpha = al_scr[slot, hh]
                acc_scr[hh, tile(qi), :] = (jnp.concatenate([alpha, alpha], axis=1)
                                            * acc_scr[hh, tile(qi), :] + part)

    blocks = [(j, j) for j in range(nq)] + [(qi, kt) for qi in range(1, nq) for kt in range(qi)]
    n = len(blocks)
    for j in range(n + 2):
        slot = j % 2
        if j >= 2:
            values(blocks[j - 2], slot, j - 2 < nq)
        if 1 <= j <= n:
            softmax(blocks[j - 1], 1 - slot, j - 1 < nq)
        if j < n:
            scores(blocks[j], slot)

    lane = lax.broadcasted_iota(jnp.int32, (t, LANES), 1)
    for qi in range(nq):
        for pair in range(nh // 2):
            outs = [acc_scr[hh, tile(qi), :LANES] / acc_scr[hh, tile(qi), LANES:]
                    for hh in (2 * pair, 2 * pair + 1)]
            o_ref[0, tile(qi), head(pair)] = jnp.where(lane < A_V, outs[0], outs[1]).astype(o_ref.dtype)


def _attn(qa, ka, va):
    B, S, _ = qa.shape
    t = min(ATT_T, S)
    nh = ATT_HEADS
    group = lambda w: pl.BlockSpec((1, S, w), lambda b, p: (b, 0, p))
    return pl.pallas_call(
        _attn_kernel,
        grid=(B, A_HEADS // nh),
        in_specs=[group(nh * LANES), group(nh * LANES), group(nh * A_V)],
        out_specs=group(nh * A_V),
        out_shape=jax.ShapeDtypeStruct((B, S, A_WIDTH), BF16),
        scratch_shapes=[pltpu.VMEM((nh, S, LANES), F32), pltpu.VMEM((nh, S, 2 * LANES), F32),
                        pltpu.VMEM((2, nh, t, t), F32), pltpu.VMEM((2, nh // 2, 2 * t, t), BF16),
                        pltpu.VMEM((2, nh, t, LANES), F32)],
        compiler_params=pltpu.CompilerParams(dimension_semantics=("parallel", "parallel"),
                                             vmem_limit_bytes=VMEM_LIMIT),
        name="attn",
    )(qa, ka, va)


def _merge_kernel(x_ref, ada_ref, hm_ref, oa_ref, gmix_ref, wg_ref, wbm_ref, wba_ref, wo_ref, h_ref):
    x = x_ref[0]
    tm, D = x.shape
    gain = gmix_ref[...] * (1.0 + _ada_rows(ada_ref, 1))
    u = (_rms(x, gain) + _rows(_ada_rows(ada_ref, 0), tm)).astype(BF16)
    y = _sigmoid(_bdot(u, wg_ref[:, 0:D])) * _bdot(hm_ref[0], wbm_ref[...])
    y = y + _sigmoid(_bdot(u, wg_ref[:, D:2 * D])) * _bdot(oa_ref[0], wba_ref[...])
    h_ref[0] = x + _rows(_ada_rows(ada_ref, 2), tm) * _bdot(y.astype(BF16), wo_ref[...])


def _merge(x, ada8, hm, oa, g_mix, w_g, w_bm, w_ba, w_o):
    B, S, D = x.shape
    tm = min(MERGE_TM, S)
    const = lambda shape: pl.BlockSpec(shape, lambda b, i: (0,) * len(shape),
                                       pipeline_mode=pl.Buffered(1))
    tok = lambda w: pl.BlockSpec((1, tm, w), lambda b, i: (b, i, 0))
    return pl.pallas_call(
        _merge_kernel,
        grid=(B, S // tm),
        in_specs=[tok(D), pl.BlockSpec((1,) + ada8.shape[1:], lambda b, i: (b, 0, 0)), tok(M_WIDTH), tok(A_WIDTH),
                  const(g_mix.shape), const(w_g.shape), const(w_bm.shape), const(w_ba.shape),
                  const(w_o.shape)],
        out_specs=tok(D),
        out_shape=jax.ShapeDtypeStruct((B, S, D), F32),
        compiler_params=pltpu.CompilerParams(dimension_semantics=("parallel", "parallel"),
                                             vmem_limit_bytes=VMEM_LIMIT),
        name="merge",
    )(x, ada8, hm, oa, g_mix, w_g, w_bm, w_ba, w_o)


def _ffn_kernel(h_ref, ada_ref, gffn_ref, gfin_ref, wi_ref, wo_ref, o_ref):
    h = h_ref[0]
    tm = h.shape[0]
    gain = gffn_ref[...] * (1.0 + _ada_rows(ada_ref, 4))
    u = (_rms(h, gain) + _rows(_ada_rows(ada_ref, 3), tm)).astype(BF16)
    acc = None
    for lo, hi in FFN_CHUNKS:
        a = _silu(_bdot(u, wi_ref[:, lo:hi])) * _bdot(u, wi_ref[:, D_FF + lo:D_FF + hi])
        part = _bdot(a.astype(BF16), wo_ref[lo:hi, :])
        acc = part if acc is None else acc + part
    o_ref[0] = _rms(h + _rows(_ada_rows(ada_ref, 5), tm) * acc, gfin_ref[...])


def _ffn(h1, ada8, g_ffn, g_final, w_in, w_out):
    B, S, D = h1.shape
    tm = min(FFN_TM, S)
    tok = pl.BlockSpec((1, tm, D), lambda b, i: (b, i, 0))
    resident = lambda shape: pl.BlockSpec(shape, lambda b, i: (0, 0), pipeline_mode=pl.Buffered(1))
    return pl.pallas_call(
        _ffn_kernel,
        grid=(B, S // tm),
        in_specs=[tok,
                  pl.BlockSpec((1,) + ada8.shape[1:], lambda b, i: (b, 0, 0)),
                  resident(g_ffn.shape), resident(g_final.shape), resident(w_in.shape),
                  resident(w_out.shape)],
        out_specs=tok,
        out_shape=jax.ShapeDtypeStruct((B, S, D), F32),
        compiler_params=pltpu.CompilerParams(dimension_semantics=("parallel", "parallel"),
                                             vmem_limit_bytes=VMEM_LIMIT),
        name="ffn",
    )(h1, ada8, g_ffn, g_final, w_in, w_out)


def _pack_weights(w_in, w_uq, w_ukv):
    D = w_in.shape[0]
    o_gate = 4 * M_WIDTH
    o_qlat = o_gate + 2 * M_HEADS
    o_kr = o_qlat + A_Q_RANK + A_KV_RANK
    o_gm = o_kr + A_ROPE
    zeros = lambda n: jnp.zeros((D, n), w_in.dtype)
    half = A_ROPE // 2
    misc = jnp.concatenate([zeros(A_NOPE), w_in[:, o_kr:o_gm], w_in[:, o_kr:o_kr + half],
                            zeros(LANES - A_QK - half)], axis=1)
    w_b = jnp.concatenate([w_in[:, :o_gate], w_in[:, o_qlat:o_kr], misc], axis=1).astype(BF16)
    w_gt = w_in[:, o_gate:o_qlat].T.astype(BF16)
    w_g = w_in[:, o_gm:].astype(BF16)
    wuq = w_uq.reshape(A_Q_RANK, A_HEADS, A_QK)
    wuq = jnp.concatenate([wuq, wuq[:, :, A_NOPE:A_NOPE + half],
                           jnp.zeros((A_Q_RANK, A_HEADS, LANES - A_QK - half), w_uq.dtype)], axis=2)
    wuq = wuq.reshape(A_Q_RANK, A_HEADS * LANES).astype(BF16)
    wkv = w_ukv.reshape(A_KV_RANK, A_HEADS, A_NOPE + A_V)
    wk = jnp.pad(wkv[:, :, :A_NOPE], ((0, 0), (0, 0), (0, LANES - A_NOPE)))
    wk = wk.reshape(A_KV_RANK, A_HEADS * LANES).astype(BF16)
    wv = wkv[:, :, A_NOPE:].reshape(A_KV_RANK, A_WIDTH).astype(BF16)
    return w_b, w_gt, w_g, wuq, wk, wv


def _rope_rows(groups):
    inv_freq = ROPE_THETA ** (-jnp.arange(0, A_ROPE, 2, dtype=F32) / A_ROPE)
    half = A_ROPE // 2
    pair = jnp.concatenate([inv_freq, inv_freq])
    rows = []
    for g in range(groups):
        row = jnp.zeros((LANES,), F32).at[A_ROPE * g:A_ROPE * (g + 1)].set(pair)
        rows.append(jnp.broadcast_to(row, (SUBLANES, LANES)))
    sign = jnp.concatenate([-jnp.ones((half,), F32), jnp.ones((half,), F32)])
    sgn = jnp.broadcast_to(jnp.tile(sign, LANES // A_ROPE), (SUBLANES, LANES))
    return jnp.concatenate(rows, axis=0), sgn


def _rep8(v):
    v = v.reshape(-1, v.shape[-1])
    return jnp.repeat(v, SUBLANES, axis=0)


def kernel(x, c, positions, w_ada, b_ada, g_mix, w_in, conv_w, conv_b, b_igate, b_fgate, g_mhead,
           g_q_lat, w_uq, g_kv_lat, w_ukv, w_branch_m, w_branch_a, w_out, g_ffn, w_ffn_in,
           w_ffn_out, g_final):
    B, S, D = x.shape
    assert w_ada.shape[0] == 1, "the final rmsnorm is fused into the (single) layer's FFN kernel"
    invf, sgn = _rope_rows(MIX_TM // LANES)
    pos3 = positions.reshape(B, S, 1)
    l = 0
    ada = _ada(c, w_ada[l], b_ada[l])
    ada8 = jnp.repeat(ada.reshape(B, 6, 1, D), SUBLANES, axis=2).reshape(B, 6 * SUBLANES, D)
    w_b, w_gt, w_g, wuq, wk, wv = _pack_weights(w_in[l], w_uq[l], w_ukv[l])
    b_col = jnp.concatenate([b_igate[l], b_fgate[l]]).astype(F32).reshape(2 * M_HEADS, 1)
    gmix = _rep8(g_mix[l])
    qm, km, vm, om, gr, qa, ka, va = _mix_in(
        x, ada8, pos3, gmix, w_b, w_gt, _rep8(conv_w[l]), _rep8(conv_b[l]), b_col,
        _rep8(g_q_lat[l]), _rep8(g_kv_lat[l]), wuq, wk, wv, invf, sgn)
    hm = _mlstm(qm, km, vm, om, gr, _rep8(g_mhead[l]))
    oa = _attn(qa, ka, va)
    h1 = _merge(x, ada8, hm, oa, gmix, w_g, w_branch_m[l].astype(BF16),
                w_branch_a[l].astype(BF16), w_out[l].astype(BF16))
    return _ffn(h1, ada8, _rep8(g_ffn[l]), _rep8(g_final), w_ffn_in[l].astype(BF16),
                w_ffn_out[l].astype(BF16))
```

```python
import functools

import jax
import jax.numpy as jnp
import numpy as np
from jax import lax
from jax.experimental import pallas as pl
from jax.experimental.pallas import tpu as pltpu

F32 = jnp.float32
BF16 = jnp.bfloat16

D_MODEL = 1024
M_HEADS = 4
M_HEAD_DIM = 128
M_WIDTH = M_HEADS * M_HEAD_DIM
M_CONV = 4
A_HEADS = 8
A_NOPE = 64
A_ROPE = 32
A_V = 64
A_QK = A_NOPE + A_ROPE
A_Q_RANK = 256
A_KV_RANK = 128
A_WIDTH = A_HEADS * A_V
ROPE_THETA = 10000.0
D_FF = 2816
NORM_EPS = 1e-6

LANES = 128
SUBLANES = 8
HALO = 8
VMEM_LIMIT = 56 * 1024 * 1024

MIX_TM = 1024
MERGE_TM = 1024
MIX_TN = 256
M_CHUNK = 128
M_UNROLL = 16
ATT_T = 256
ATT_HEADS = 2
LOG2E = 1.4426950408889634
FFN_TM = 1024
MXU_N = 256
FFN_CHUNKS = ((0, 6 * MXU_N), (6 * MXU_N, D_FF))


def _sigmoid(x):
    return 0.5 * jnp.tanh(0.5 * x) + 0.5


def _silu(x):
    return x * _sigmoid(x)


def _log_sigmoid(x):
    return jnp.minimum(x, 0.0) - jnp.log1p(jnp.exp(-jnp.abs(x)))


def _rows(v8, n):
    return v8 if n == SUBLANES else jnp.concatenate([v8] * (n // SUBLANES), axis=0)


def _rms(x, g8):
    n, d = x.shape
    ms = jnp.broadcast_to(jnp.sum(x * x, axis=-1, keepdims=True), (n, LANES)) * (1.0 / d)
    r = lax.rsqrt(ms + NORM_EPS)
    if d > LANES:
        r = jnp.concatenate([r] * (d // LANES), axis=1)
    return x * r * _rows(g8, n)


def _ada_rows(ada_ref, k):
    return ada_ref[0, SUBLANES * k:SUBLANES * (k + 1), :]


def _bdot(a, b):
    return jnp.dot(a, b, preferred_element_type=F32)


def _dot_nt(a, b):
    return lax.dot_general(a, b, (((1,), (1,)), ((), ())), preferred_element_type=F32)


def _ada_kernel(c_ref, w_ref, b_ref, o_ref):
    c = c_ref[...]
    o_ref[...] = jnp.dot(_silu(c), w_ref[...], preferred_element_type=F32,
                         precision=lax.Precision.HIGHEST) + b_ref[...]


def _ada(c, w_ada, b_ada):
    B, D = c.shape
    N = w_ada.shape[1]
    return pl.pallas_call(
        _ada_kernel,
        grid=(N // D,),
        in_specs=[pl.BlockSpec((B, D), lambda j: (0, 0)),
                  pl.BlockSpec((D, D), lambda j: (0, j)),
                  pl.BlockSpec((1, D), lambda j: (0, j))],
        out_specs=pl.BlockSpec((B, D), lambda j: (0, j)),
        out_shape=jax.ShapeDtypeStruct((B, N), F32),
        compiler_params=pltpu.CompilerParams(dimension_semantics=("arbitrary",),
                                             vmem_limit_bytes=VMEM_LIMIT),
        name="ada",
    )(c, w_ada, b_ada.reshape(1, N))


def _rope_partner(blk):
    return pltpu.roll(blk, LANES - A_ROPE // 2, 1)


def _mix_in_kernel(x_ref, xh_ref, ada_ref, pos_ref, gmix_ref, w_ref, wgt_ref, cw_ref, cb_ref,
                   bcol_ref, gq_ref, gkv_ref, wuq_ref, wk_ref, wv_ref, invf_ref, sgn_ref,
                   qm_ref, km_ref, vm_ref, om_ref, gr_ref, qa_ref, ka_ref, va_ref, z_scr):
    i = pl.program_id(1)
    tm = x_ref.shape[1]
    sh1 = _ada_rows(ada_ref, 0)
    gain = gmix_ref[...] * (1.0 + _ada_rows(ada_ref, 1))

    def modulate(xv):
        return _rms(xv, gain) + _rows(sh1, xv.shape[0])

    u = modulate(x_ref[0]).astype(BF16)
    uh = modulate(xh_ref[0]).astype(BF16)

    lat = _bdot(u, w_ref[:, 4 * M_WIDTH:4 * M_WIDTH + 4 * LANES])
    q_lat = lat[:, 0:A_Q_RANK]
    kv_lat = lat[:, A_Q_RANK:A_Q_RANK + A_KV_RANK]
    misc = lat[:, A_Q_RANK + A_KV_RANK:]

    pos = pos_ref[0].astype(F32)
    lane_groups = LANES // A_ROPE
    lane_g = lax.broadcasted_iota(jnp.int32, (LANES, LANES), 1)
    rope_g = (lane_g >= A_NOPE) & (lane_g < A_QK)
    cos_parts, sin_parts = [], []
    for pack in range(tm // (LANES * lane_groups)):
        ang = None
        for lg in range(lane_groups):
            gi = pack * lane_groups + lg
            term = (jnp.broadcast_to(pos[gi * LANES:(gi + 1) * LANES], (LANES, LANES))
                    * _rows(invf_ref[SUBLANES * lg:SUBLANES * (lg + 1), :], LANES))
            ang = term if ang is None else ang + term
        cos_p = jnp.cos(ang)
        sin_p = jnp.sin(ang) * _rows(sgn_ref[...], LANES)
        for lg in range(lane_groups):
            shift = (A_NOPE - A_ROPE * lg) % LANES
            cg = pltpu.roll(cos_p, shift, 1) if shift else cos_p
            sg = pltpu.roll(sin_p, shift, 1) if shift else sin_p
            cos_parts.append(jnp.where(rope_g, cg, 1.0))
            sin_parts.append(jnp.where(rope_g, sg, 0.0))
    cosv = jnp.concatenate(cos_parts, axis=0)
    sinv = jnp.concatenate(sin_parts, axis=0)
    lane = lax.broadcasted_iota(jnp.int32, (tm, LANES), 1)
    is_rope = (lane >= A_NOPE) & (lane < A_QK)

    cq = _rms(q_lat, gq_ref[...]).astype(BF16)
    ckv = _rms(kv_lat, gkv_ref[...]).astype(BF16)
    va_ref[0] = _bdot(ckv, wv_ref[...]).astype(BF16)
    krot = jnp.where(is_rope, misc * cosv + _rope_partner(misc) * sinv, 0.0)
    scale = A_QK ** -0.5 * LOG2E

    assert 2 * M_WIDTH // MIX_TN == A_HEADS // 2
    for cblk in range(2 * M_WIDTH // MIX_TN):
        cols = slice(cblk * MIX_TN, (cblk + 1) * MIX_TN)
        z_scr[0:HALO, cols] = jnp.where(i > 0, _bdot(uh, w_ref[:, cols]), 0.0)
        z_scr[HALO:HALO + tm, cols] = _bdot(u, w_ref[:, cols])
        y = _rows(cb_ref[:, cols], tm)
        for j in range(M_CONV):
            sft = M_CONV - 1 - j
            y = y + (z_scr[HALO - sft:HALO - sft + tm, cols]
                     * _rows(cw_ref[SUBLANES * j:SUBLANES * (j + 1), cols], tm))
        qk = _silu(y)
        if cblk < M_WIDTH // MIX_TN:
            qm_ref[0, :, cols] = (qk * (M_HEAD_DIM ** -0.5)).astype(BF16)
        else:
            km_ref[0, :, cblk * MIX_TN - M_WIDTH:(cblk + 1) * MIX_TN - M_WIDTH] = qk.astype(BF16)

        qa = _bdot(cq, wuq_ref[:, cols])
        kn = _bdot(ckv, wk_ref[:, cols])
        for hh in range(2):
            hcols = slice((2 * cblk + hh) * LANES, (2 * cblk + hh + 1) * LANES)
            blk = qa[:, hh * LANES:(hh + 1) * LANES]
            rot = blk * cosv + _rope_partner(blk) * sinv
            qa_ref[0, :, hcols] = (rot * scale).astype(BF16)
            ka_ref[0, :, hcols] = (kn[:, hh * LANES:(hh + 1) * LANES] + krot).astype(BF16)

        vo = _bdot(u, w_ref[:, 2 * M_WIDTH + cblk * MIX_TN:2 * M_WIDTH + (cblk + 1) * MIX_TN])
        if cblk < M_WIDTH // MIX_TN:
            vm_ref[0, :, cols] = vo.astype(BF16)
        else:
            om_ref[0, :, cblk * MIX_TN - M_WIDTH:(cblk + 1) * MIX_TN - M_WIDTH] = vo

    gpre = _dot_nt(wgt_ref[...], u) + bcol_ref[...]
    row = lax.broadcasted_iota(jnp.int32, gpre.shape, 0)
    gates = jnp.where(row < M_HEADS, gpre, _log_sigmoid(gpre))
    for cix in range(tm // M_CHUNK):
        gr_ref[0, 2 * M_HEADS * cix:2 * M_HEADS * (cix + 1), :] = gates[:, cix * M_CHUNK:(cix + 1) * M_CHUNK]


def _mix_in(x, ada8, pos3, g_mix, w_b, w_gt, conv_w, conv_b, b_col, g_q, g_kv, wuq, wk, wv,
            invf, sgn):
    B, S, D = x.shape
    tm = min(MIX_TM, S)
    nt = S // tm
    hb = tm // HALO
    const = lambda shape: pl.BlockSpec(shape, lambda b, i: (0,) * len(shape))
    tok = lambda w: pl.BlockSpec((1, tm, w), lambda b, i: (b, i, 0))
    out_shapes = (
        jax.ShapeDtypeStruct((B, S, M_WIDTH), BF16),
        jax.ShapeDtypeStruct((B, S, M_WIDTH), BF16),
        jax.ShapeDtypeStruct((B, S, M_WIDTH), BF16),
        jax.ShapeDtypeStruct((B, S, M_WIDTH), F32),
        jax.ShapeDtypeStruct((B, 2 * M_HEADS * (S // M_CHUNK), M_CHUNK), F32),
        jax.ShapeDtypeStruct((B, S, A_HEADS * LANES), BF16),
        jax.ShapeDtypeStruct((B, S, A_HEADS * LANES), BF16),
        jax.ShapeDtypeStruct((B, S, A_WIDTH), BF16),
    )
    return pl.pallas_call(
        _mix_in_kernel,
        grid=(B, nt),
        in_specs=[tok(D),
                  pl.BlockSpec((1, HALO, D), lambda b, i: (b, jnp.maximum(i * hb - 1, 0), 0)),
                  pl.BlockSpec((1,) + ada8.shape[1:], lambda b, i: (b, 0, 0)),
                  tok(1),
                  const(g_mix.shape), const(w_b.shape), const(w_gt.shape), const(conv_w.shape),
                  const(conv_b.shape), const(b_col.shape), const(g_q.shape), const(g_kv.shape),
                  const(wuq.shape), const(wk.shape), const(wv.shape), const(invf.shape),
                  const(sgn.shape)],
        out_specs=(tok(M_WIDTH), tok(M_WIDTH), tok(M_WIDTH), tok(M_WIDTH),
                   pl.BlockSpec((1, 2 * M_HEADS * (tm // M_CHUNK), M_CHUNK), lambda b, i: (b, i, 0)),
                   tok(A_HEADS * LANES), tok(A_HEADS * LANES), tok(A_WIDTH)),
        out_shape=out_shapes,
        scratch_shapes=[pltpu.VMEM((HALO + tm, 2 * M_WIDTH), F32)],
        compiler_params=pltpu.CompilerParams(dimension_semantics=("parallel", "arbitrary"),
                                             vmem_limit_bytes=VMEM_LIMIT),
        name="mix_in",
    )(x, x, ada8, pos3, g_mix, w_b, w_gt, conv_w, conv_b, b_col, g_q, g_kv, wuq, wk, wv, invf, sgn)


def _aligned(i, m):
    return i if isinstance(i, int) else pl.multiple_of(i, m)


def _split3(x):
    hi = x.astype(BF16)
    r = x - hi.astype(F32)
    mid = r.astype(BF16)
    lo = (r - mid.astype(F32)).astype(BF16)
    return hi, mid, lo


def _mlstm_kernel(q_ref, k_ref, v_ref, om_ref, gr_ref, gmh_ref, uo_ref, o_ref,
                  c_scr, rows_scr, dw_scr, iw_scr, em_scr, wc_scr):
    S = q_ref.shape[1]
    L = M_CHUNK
    dh = M_HEAD_DIM
    H = M_HEADS
    nc = S // L
    R = 2 * H * nc

    g = gr_ref[0] * LOG2E
    prod = _bdot(jnp.concatenate(_split3(g), axis=0), uo_ref[...])
    prod = prod[:R] + prod[R:2 * R] + prod[2 * R:]
    cs = prod[:, :L]
    tot = prod[:, L:]
    li = pltpu.roll(g, H, 0)
    wl = (tot - cs) + li
    wmax = jnp.broadcast_to(jnp.max(wl, axis=-1, keepdims=True), (R, L))
    m_cur = jnp.zeros((2 * H, L), F32)
    m_prev, m_new = [], []
    for c in range(nc):
        rs = slice(2 * H * c, 2 * H * (c + 1))
        m_prev.append(m_cur)
        m_cur = jnp.maximum(tot[rs] + m_cur, wmax[rs])
        m_new.append(m_cur)
    m_prev = jnp.concatenate(m_prev, axis=0)
    m_new = jnp.concatenate(m_new, axis=0)
    rows_scr[0] = cs
    rows_scr[1] = li
    rows_scr[2] = wl
    rows_scr[3] = m_prev
    rows_scr[4] = m_new
    rows_scr[5] = jnp.exp2(tot + m_prev - m_new)

    causal = (lax.broadcasted_iota(jnp.int32, (L, L), 0) >= lax.broadcasted_iota(jnp.int32, (L, L), 1))
    upper = lax.broadcasted_iota(jnp.int32, (2 * H, L), 0) < H

    def prep(c, slot):
        rs = pl.ds(_aligned(c * 2 * H, 2 * H), 2 * H)
        cs_c, li_c, wl_c = rows_scr[0, rs, :], rows_scr[1, rs, :], rows_scr[2, rs, :]
        mp_c, mn_c = rows_scr[3, rs, :], rows_scr[4, rs, :]
        z = jnp.where(upper, pltpu.roll(wl_c, H, 0), cs_c)
        cols = jnp.concatenate([z] * (L // (2 * H)), axis=0).T
        for h in range(H):
            wl_col = jnp.broadcast_to(cols[:, h:h + 1], (L, L))
            b_col = jnp.broadcast_to(cols[:, H + h:H + h + 1], (L, L))
            r = H + h
            dlog = jnp.where(causal, (b_col - cs_c[r:r + 1, :]) + li_c[r:r + 1, :], -jnp.inf)
            inter_log = b_col + mp_c[r:r + 1, :]
            m_t = jnp.maximum(inter_log, jnp.max(dlog, axis=-1, keepdims=True))
            dw_scr[slot, h] = jnp.exp2(dlog - m_t)
            iw_scr[slot, h] = jnp.exp2(inter_log - m_t)
            em_scr[slot, h] = jnp.exp2(-m_t)
            wc_scr[slot, h] = jnp.exp2(wl_col - mn_c[r:r + 1, :])

    c_scr[...] = jnp.zeros_like(c_scr)
    ones = jnp.ones((L, dh), BF16)

    def chunk(c, slot):
        ts = pl.ds(_aligned(c * L, L), L)
        for h in range(H):
            hs = slice(h * dh, (h + 1) * dh)
            q = q_ref[0, ts, hs]
            k = k_ref[0, ts, hs]
            v_aug = jnp.concatenate([v_ref[0, ts, hs], ones], axis=1)
            c_aug = c_scr[h]
            iw = iw_scr[slot, h]
            s = _dot_nt(q, k) * dw_scr[slot, h]
            qc = _bdot(q, c_aug.astype(BF16))
            sv = _bdot(s.astype(BF16), v_aug)
            num = sv[:, :dh] + iw * qc[:, :dh]
            den = sv[:, dh:] + iw * qc[:, dh:]
            hv = num / jnp.maximum(jnp.abs(den), em_scr[slot, h])
            hn = _rms(hv, gmh_ref[:, hs])
            o_ref[0, ts, hs] = (_sigmoid(om_ref[0, ts, hs]) * hn).astype(o_ref.dtype)

            decay = rows_scr[5, pl.ds(c * 2 * H + H + h, 1), :]
            kw_t = (k.astype(F32) * wc_scr[slot, h]).T.astype(BF16)
            c_scr[h] = jnp.concatenate([decay, decay], axis=1) * c_aug + _bdot(kw_t, v_aug)

    unroll = M_UNROLL if nc % M_UNROLL == 0 else 2
    prep(0, 0)
    if unroll == nc:
        for c in range(nc):
            chunk(c, c % 2)
            if c + 1 < nc:
                prep(c + 1, (c + 1) % 2)
        return

    def body(i, carry):
        c = unroll * i
        for k in range(unroll):
            chunk(c + k, k % 2)
            prep(jnp.minimum(c + k + 1, nc - 1), (k + 1) % 2)
        return carry

    lax.fori_loop(0, nc // unroll, body, 0)


def _mlstm(qm, km, vm, om, gr, g_mhead):
    B, S, W = qm.shape
    L = M_CHUNK
    uo = np.concatenate([np.triu(np.ones((L, L), np.float32)), np.ones((L, L), np.float32)], axis=1)
    seq = lambda: pl.BlockSpec((1, S, W), lambda b: (b, 0, 0))
    rows = 2 * M_HEADS * (S // L)
    assert (S // L) % 2 == 0, "the chunk loop handles two chunks per iteration"
    per_head = lambda: pltpu.VMEM((2, M_HEADS, L, L), F32)
    return pl.pallas_call(
        _mlstm_kernel,
        grid=(B,),
        in_specs=[seq(), seq(), seq(), seq(),
                  pl.BlockSpec((1, rows, L), lambda b: (b, 0, 0)),
                  pl.BlockSpec((SUBLANES, W), lambda b: (0, 0)),
                  pl.BlockSpec((L, 2 * L), lambda b: (0, 0))],
        out_specs=seq(),
        out_shape=jax.ShapeDtypeStruct((B, S, W), BF16),
        scratch_shapes=[pltpu.VMEM((M_HEADS, M_HEAD_DIM, 2 * M_HEAD_DIM), F32),
                        pltpu.VMEM((6, rows, L), F32),
                        per_head(), per_head(), per_head(), per_head()],
        compiler_params=pltpu.CompilerParams(dimension_semantics=("parallel",),
                                             vmem_limit_bytes=VMEM_LIMIT),
        name="mlstm",
    )(qm, km, vm, om, gr, g_mhead, jnp.asarray(uo, BF16))


def _attn_kernel(q_ref, k_ref, v_ref, o_ref, m_scr, acc_scr, s_scr, p_scr, al_scr):
    S = q_ref.shape[1]
    t = min(ATT_T, S)
    nq = S // t
    nh = ATT_HEADS
    causal = (lax.broadcasted_iota(jnp.int32, (t, t), 1)
              <= lax.broadcasted_iota(jnp.int32, (t, t), 0))

    def tile(i):
        return slice(i * t, (i + 1) * t)

    def head(hh):
        return slice(hh * LANES, (hh + 1) * LANES)

    def scores(blk, slot):
        qi, kt = blk
        for hh in range(nh):
            s_scr[slot, hh] = _dot_nt(q_ref[0, tile(qi), head(hh)], k_ref[0, tile(kt), head(hh)])

    def softmax(blk, slot, diag):
        qi, _ = blk
        for hh in range(nh):
            mask = (lambda x: jnp.where(causal, x, -jnp.inf)) if diag else (lambda x: x)
            row_max = jnp.max(mask(s_scr[slot, hh]), axis=-1, keepdims=True)
            if diag:
                m_new = jnp.broadcast_to(row_max, (t, LANES))
            else:
                m_old = m_scr[hh, tile(qi), :]
                m_new = jnp.maximum(m_old, row_max)
                al_scr[slot, hh] = jnp.exp2(m_old - m_new)
            m_scr[hh, tile(qi), :] = m_new
            p = jnp.exp2(mask(s_scr[slot, hh]) - jnp.concatenate([m_new] * (t // LANES), axis=1))
            p_scr[slot, hh // 2, (hh % 2) * t:(hh % 2 + 1) * t, :] = p.astype(BF16)

    ones = jnp.ones((t, LANES), BF16)

    def values(blk, slot, diag):
        qi, kt = blk
        for hh in range(nh):
            if hh % 2 == 0:
                pv = _bdot(p_scr[slot, hh // 2],
                           jnp.concatenate([v_ref[0, tile(kt), head(hh // 2)], ones], axis=1))
            part = pv[(hh % 2) * t:(hh % 2 + 1) * t]
            if diag:
                acc_scr[hh, tile(qi), :] = part
            else:
                alpha = al_scr[slot, hh]
                acc_scr[hh, tile(qi), :] = (jnp.concatenate([alpha, alpha], axis=1)
                                            * acc_scr[hh, tile(qi), :] + part)

    blocks = [(j, j) for j in range(nq)] + [(qi, kt) for qi in range(1, nq) for kt in range(qi)]
    n = len(blocks)
    for j in range(n + 2):
        slot = j % 2
        if j >= 2:
            values(blocks[j - 2], slot, j - 2 < nq)
        if 1 <= j <= n:
            softmax(blocks[j - 1], 1 - slot, j - 1 < nq)
        if j < n:
            scores(blocks[j], slot)

    lane = lax.broadcasted_iota(jnp.int32, (t, LANES), 1)
    for qi in range(nq):
        for pair in range(nh // 2):
            outs = [acc_scr[hh, tile(qi), :LANES] / acc_scr[hh, tile(qi), LANES:]
                    for hh in (2 * pair, 2 * pair + 1)]
            o_ref[0, tile(qi), head(pair)] = jnp.where(lane < A_V, outs[0], outs[1]).astype(o_ref.dtype)


def _attn(qa, ka, va):
    B, S, _ = qa.shape
    t = min(ATT_T, S)
    nh = ATT_HEADS
    group = lambda w: pl.BlockSpec((1, S, w), lambda b, p: (b, 0, p))
    return pl.pallas_call(
        _attn_kernel,
        grid=(B, A_HEADS // nh),
        in_specs=[group(nh * LANES), group(nh * LANES), group(nh * A_V)],
        out_specs=group(nh * A_V),
        out_shape=jax.ShapeDtypeStruct((B, S, A_WIDTH), BF16),
        scratch_shapes=[pltpu.VMEM((nh, S, LANES), F32), pltpu.VMEM((nh, S, 2 * LANES), F32),
                        pltpu.VMEM((2, nh, t, t), F32), pltpu.VMEM((2, nh // 2, 2 * t, t), BF16),
                        pltpu.VMEM((2, nh, t, LANES), F32)],
        compiler_params=pltpu.CompilerParams(dimension_semantics=("parallel", "parallel"),
                                             vmem_limit_bytes=VMEM_LIMIT),
        name="attn",
    )(qa, ka, va)


def _merge_kernel(x_ref, ada_ref, hm_ref, oa_ref, gmix_ref, wg_ref, wbm_ref, wba_ref, wo_ref, h_ref):
    x = x_ref[0]
    tm, D = x.shape
    gain = gmix_ref[...] * (1.0 + _ada_rows(ada_ref, 1))
    u = (_rms(x, gain) + _rows(_ada_rows(ada_ref, 0), tm)).astype(BF16)
    y = _sigmoid(_bdot(u, wg_ref[:, 0:D])) * _bdot(hm_ref[0], wbm_ref[...])
    y = y + _sigmoid(_bdot(u, wg_ref[:, D:2 * D])) * _bdot(oa_ref[0], wba_ref[...])
    h_ref[0] = x + _rows(_ada_rows(ada_ref, 2), tm) * _bdot(y.astype(BF16), wo_ref[...])


def _merge(x, ada8, hm, oa, g_mix, w_g, w_bm, w_ba, w_o):
    B, S, D = x.shape
    tm = min(MERGE_TM, S)
    const = lambda shape: pl.BlockSpec(shape, lambda b, i: (0,) * len(shape),
                                       pipeline_mode=pl.Buffered(1))
    tok = lambda w: pl.BlockSpec((1, tm, w), lambda b, i: (b, i, 0))
    return pl.pallas_call(
        _merge_kernel,
        grid=(B, S // tm),
        in_specs=[tok(D), pl.BlockSpec((1,) + ada8.shape[1:], lambda b, i: (b, 0, 0)), tok(M_WIDTH), tok(A_WIDTH),
                  const(g_mix.shape), const(w_g.shape), const(w_bm.shape), const(w_ba.shape),
                  const(w_o.shape)],
        out_specs=tok(D),
        out_shape=jax.ShapeDtypeStruct((B, S, D), F32),
        compiler_params=pltpu.CompilerParams(dimension_semantics=("parallel", "parallel"),
                                             vmem_limit_bytes=VMEM_LIMIT),
        name="merge",
    )(x, ada8, hm, oa, g_mix, w_g, w_bm, w_ba, w_o)


def _ffn_kernel(h_ref, ada_ref, gffn_ref, gfin_ref, wi_ref, wo_ref, o_ref):
    h = h_ref[0]
    tm = h.shape[0]
    gain = gffn_ref[...] * (1.0 + _ada_rows(ada_ref, 4))
    u = (_rms(h, gain) + _rows(_ada_rows(ada_ref, 3), tm)).astype(BF16)
    acc = None
    for lo, hi in FFN_CHUNKS:
        a = _silu(_bdot(u, wi_ref[:, lo:hi])) * _bdot(u, wi_ref[:, D_FF + lo:D_FF + hi])
        part = _bdot(a.astype(BF16), wo_ref[lo:hi, :])
        acc = part if acc is None else acc + part
    o_ref[0] = _rms(h + _rows(_ada_rows(ada_ref, 5), tm) * acc, gfin_ref[...])


def _ffn(h1, ada8, g_ffn, g_final, w_in, w_out):
    B, S, D = h1.shape
    tm = min(FFN_TM, S)
    tok = pl.BlockSpec((1, tm, D), lambda b, i: (b, i, 0))
    resident = lambda shape: pl.BlockSpec(shape, lambda b, i: (0, 0), pipeline_mode=pl.Buffered(1))
    return pl.pallas_call(
        _ffn_kernel,
        grid=(B, S // tm),
        in_specs=[tok,
                  pl.BlockSpec((1,) + ada8.shape[1:], lambda b, i: (b, 0, 0)),
                  resident(g_ffn.shape), resident(g_final.shape), resident(w_in.shape),
                  resident(w_out.shape)],
        out_specs=tok,
        out_shape=jax.ShapeDtypeStruct((B, S, D), F32),
        compiler_params=pltpu.CompilerParams(dimension_semantics=("parallel", "parallel"),
                                             vmem_limit_bytes=VMEM_LIMIT),
        name="ffn",
    )(h1, ada8, g_ffn, g_final, w_in, w_out)


def _pack_weights(w_in, w_uq, w_ukv):
    D = w_in.shape[0]
    o_gate = 4 * M_WIDTH
    o_qlat = o_gate + 2 * M_HEADS
    o_kr = o_qlat + A_Q_RANK + A_KV_RANK
    o_gm = o_kr + A_ROPE
    zeros = lambda n: jnp.zeros((D, n), w_in.dtype)
    half = A_ROPE // 2
    misc = jnp.concatenate([zeros(A_NOPE), w_in[:, o_kr:o_gm], w_in[:, o_kr:o_kr + half],
                            zeros(LANES - A_QK - half)], axis=1)
    w_b = jnp.concatenate([w_in[:, :o_gate], w_in[:, o_qlat:o_kr], misc], axis=1).astype(BF16)
    w_gt = w_in[:, o_gate:o_qlat].T.astype(BF16)
    w_g = w_in[:, o_gm:].astype(BF16)
    wuq = w_uq.reshape(A_Q_RANK, A_HEADS, A_QK)
    wuq = jnp.concatenate([wuq, wuq[:, :, A_NOPE:A_NOPE + half],
                           jnp.zeros((A_Q_RANK, A_HEADS, LANES - A_QK - half), w_uq.dtype)], axis=2)
    wuq = wuq.reshape(A_Q_RANK, A_HEADS * LANES).astype(BF16)
    wkv = w_ukv.reshape(A_KV_RANK, A_HEADS, A_NOPE + A_V)
    wk = jnp.pad(wkv[:, :, :A_NOPE], ((0, 0), (0, 0), (0, LANES - A_NOPE)))
    wk = wk.reshape(A_KV_RANK, A_HEADS * LANES).astype(BF16)
    wv = wkv[:, :, A_NOPE:].reshape(A_KV_RANK, A_WIDTH).astype(BF16)
    return w_b, w_gt, w_g, wuq, wk, wv


def _rope_rows(groups):
    inv_freq = ROPE_THETA ** (-jnp.arange(0, A_ROPE, 2, dtype=F32) / A_ROPE)
    half = A_ROPE // 2
    pair = jnp.concatenate([inv_freq, inv_freq])
    rows = []
    for g in range(groups):
        row = jnp.zeros((LANES,), F32).at[A_ROPE * g:A_ROPE * (g + 1)].set(pair)
        rows.append(jnp.broadcast_to(row, (SUBLANES, LANES)))
    sign = jnp.concatenate([-jnp.ones((half,), F32), jnp.ones((half,), F32)])
    sgn = jnp.broadcast_to(jnp.tile(sign, LANES // A_ROPE), (SUBLANES, LANES))
    return jnp.concatenate(rows, axis=0), sgn


def _rep8(v):
    v = v.reshape(-1, v.shape[-1])
    return jnp.repeat(v, SUBLANES, axis=0)


def kernel(x, c, positions, w_ada, b_ada, g_mix, w_in, conv_w, conv_b, b_igate, b_fgate, g_mhead,
           g_q_lat, w_uq, g_kv_lat, w_ukv, w_branch_m, w_branch_a, w_out, g_ffn, w_ffn_in,
           w_ffn_out, g_final):
    B, S, D = x.shape
    assert w_ada.shape[0] == 1, "the final rmsnorm is fused into the (single) layer's FFN kernel"
    invf, sgn = _rope_rows(LANES // A_ROPE)
    pos3 = positions.reshape(B, S, 1)
    l = 0
    ada = _ada(c, w_ada[l], b_ada[l])
    ada8 = jnp.repeat(ada.reshape(B, 6, 1, D), SUBLANES, axis=2).reshape(B, 6 * SUBLANES, D)
    w_b, w_gt, w_g, wuq, wk, wv = _pack_weights(w_in[l], w_uq[l], w_ukv[l])
    b_col = jnp.concatenate([b_igate[l], b_fgate[l]]).astype(F32).reshape(2 * M_HEADS, 1)
    gmix = _rep8(g_mix[l])
    qm, km, vm, om, gr, qa, ka, va = _mix_in(
        x, ada8, pos3, gmix, w_b, w_gt, _rep8(conv_w[l]), _rep8(conv_b[l]), b_col,
        _rep8(g_q_lat[l]), _rep8(g_kv_lat[l]), wuq, wk, wv, invf, sgn)
    hm = _mlstm(qm, km, vm, om, gr, _rep8(g_mhead[l]))
    oa = _attn(qa, ka, va)
    h1 = _merge(x, ada8, hm, oa, gmix, w_g, w_branch_m[l].astype(BF16),
                w_branch_a[l].astype(BF16), w_out[l].astype(BF16))
    return _ffn(h1, ada8, _rep8(g_ffn[l]), _rep8(g_final), w_ffn_in[l].astype(BF16),
                w_ffn_out[l].astype(BF16))
```

```python
import functools

import jax
import jax.numpy as jnp
import numpy as np
from jax import lax
from jax.experimental import pallas as pl
from jax.experimental.pallas import tpu as pltpu

F32 = jnp.float32
BF16 = jnp.bfloat16

D_MODEL = 1024
M_HEADS = 4
M_HEAD_DIM = 128
M_WIDTH = M_HEADS * M_HEAD_DIM
M_CONV = 4
A_HEADS = 8
A_NOPE = 64
A_ROPE = 32
A_V = 64
A_QK = A_NOPE + A_ROPE
A_Q_RANK = 256
A_KV_RANK = 128
A_WIDTH = A_HEADS * A_V
ROPE_THETA = 10000.0
D_FF = 2816
NORM_EPS = 1e-6

LANES = 128
SUBLANES = 8
HALO = 8
VMEM_LIMIT = 56 * 1024 * 1024

MIX_TM = 1024
MERGE_TM = 1024
MIX_TN = 256
M_CHUNK = 128
M_UNROLL = 2
ATT_T = 256
ATT_HEADS = 2
LOG2E = 1.4426950408889634
FFN_TM = 1024
MXU_N = 256
FFN_CHUNKS = ((0, 6 * MXU_N), (6 * MXU_N, D_FF))


def _sigmoid(x):
    return 0.5 * jnp.tanh(0.5 * x) + 0.5


def _silu(x):
    return x * _sigmoid(x)


def _log_sigmoid(x):
    return jnp.minimum(x, 0.0) - jnp.log1p(jnp.exp(-jnp.abs(x)))


def _rows(v8, n):
    return v8 if n == SUBLANES else jnp.concatenate([v8] * (n // SUBLANES), axis=0)


def _rms(x, g8):
    n, d = x.shape
    ms = jnp.broadcast_to(jnp.sum(x * x, axis=-1, keepdims=True), (n, LANES)) * (1.0 / d)
    r = lax.rsqrt(ms + NORM_EPS)
    if d > LANES:
        r = jnp.concatenate([r] * (d // LANES), axis=1)
    return x * r * _rows(g8, n)


def _ada_rows(ada_ref, k):
    return ada_ref[0, SUBLANES * k:SUBLANES * (k + 1), :]


def _bdot(a, b):
    return jnp.dot(a, b, preferred_element_type=F32)


def _dot_nt(a, b):
    return lax.dot_general(a, b, (((1,), (1,)), ((), ())), preferred_element_type=F32)


def _ada_kernel(c_ref, w_ref, b_ref, o_ref):
    c = c_ref[...]
    o_ref[...] = jnp.dot(_silu(c), w_ref[...], preferred_element_type=F32,
                         precision=lax.Precision.HIGHEST) + b_ref[...]


def _ada(c, w_ada, b_ada):
    B, D = c.shape
    N = w_ada.shape[1]
    return pl.pallas_call(
        _ada_kernel,
        grid=(N // D,),
        in_specs=[pl.BlockSpec((B, D), lambda j: (0, 0)),
                  pl.BlockSpec((D, D), lambda j: (0, j)),
                  pl.BlockSpec((1, D), lambda j: (0, j))],
        out_specs=pl.BlockSpec((B, D), lambda j: (0, j)),
        out_shape=jax.ShapeDtypeStruct((B, N), F32),
        compiler_params=pltpu.CompilerParams(dimension_semantics=("arbitrary",),
                                             vmem_limit_bytes=VMEM_LIMIT),
        name="ada",
    )(c, w_ada, b_ada.reshape(1, N))


def _rope_partner(blk):
    return pltpu.roll(blk, LANES - A_ROPE // 2, 1)


def _mix_in_kernel(x_ref, xh_ref, ada_ref, pos_ref, gmix_ref, w_ref, wgt_ref, cw_ref, cb_ref,
                   bcol_ref, gq_ref, gkv_ref, wuq_ref, wk_ref, wv_ref, invf_ref, sgn_ref,
                   qm_ref, km_ref, vm_ref, om_ref, gr_ref, qa_ref, ka_ref, va_ref, z_scr):
    i = pl.program_id(1)
    tm = x_ref.shape[1]
    sh1 = _ada_rows(ada_ref, 0)
    gain = gmix_ref[...] * (1.0 + _ada_rows(ada_ref, 1))

    def modulate(xv):
        return _rms(xv, gain) + _rows(sh1, xv.shape[0])

    u = modulate(x_ref[0]).astype(BF16)
    uh = modulate(xh_ref[0]).astype(BF16)

    lat = _bdot(u, w_ref[:, 4 * M_WIDTH:4 * M_WIDTH + 4 * LANES])
    q_lat = lat[:, 0:A_Q_RANK]
    kv_lat = lat[:, A_Q_RANK:A_Q_RANK + A_KV_RANK]
    misc = lat[:, A_Q_RANK + A_KV_RANK:]

    pos = pos_ref[0].astype(F32)
    lane_groups = LANES // A_ROPE
    lane_g = lax.broadcasted_iota(jnp.int32, (LANES, LANES), 1)
    rope_g = (lane_g >= A_NOPE) & (lane_g < A_QK)
    cos_parts, sin_parts = [], []
    for pack in range(tm // (LANES * lane_groups)):
        ang = None
        for lg in range(lane_groups):
            gi = pack * lane_groups + lg
            by_row = jnp.broadcast_to(pos[gi:gi + 1, :], (LANES, LANES)).T
            term = by_row * _rows(invf_ref[SUBLANES * lg:SUBLANES * (lg + 1), :], LANES)
            ang = term if ang is None else ang + term
        cos_p = jnp.cos(ang)
        sin_p = jnp.sin(ang) * _rows(sgn_ref[...], LANES)
        for lg in range(lane_groups):
            shift = (A_NOPE - A_ROPE * lg) % LANES
            cg = pltpu.roll(cos_p, shift, 1) if shift else cos_p
            sg = pltpu.roll(sin_p, shift, 1) if shift else sin_p
            cos_parts.append(jnp.where(rope_g, cg, 1.0))
            sin_parts.append(jnp.where(rope_g, sg, 0.0))
    cosv = jnp.concatenate(cos_parts, axis=0)
    sinv = jnp.concatenate(sin_parts, axis=0)
    lane = lax.broadcasted_iota(jnp.int32, (tm, LANES), 1)
    is_rope = (lane >= A_NOPE) & (lane < A_QK)

    cq = _rms(q_lat, gq_ref[...]).astype(BF16)
    ckv = _rms(kv_lat, gkv_ref[...]).astype(BF16)
    va_ref[0] = _bdot(ckv, wv_ref[...]).astype(BF16)
    krot = jnp.where(is_rope, misc * cosv + _rope_partner(misc) * sinv, 0.0)
    scale = A_QK ** -0.5 * LOG2E

    assert 2 * M_WIDTH // MIX_TN == A_HEADS // 2
    for cblk in range(2 * M_WIDTH // MIX_TN):
        cols = slice(cblk * MIX_TN, (cblk + 1) * MIX_TN)
        z_scr[0:HALO, cols] = jnp.where(i > 0, _bdot(uh, w_ref[:, cols]), 0.0)
        z_scr[HALO:HALO + tm, cols] = _bdot(u, w_ref[:, cols])
        y = _rows(cb_ref[:, cols], tm)
        for j in range(M_CONV):
            sft = M_CONV - 1 - j
            y = y + (z_scr[HALO - sft:HALO - sft + tm, cols]
                     * _rows(cw_ref[SUBLANES * j:SUBLANES * (j + 1), cols], tm))
        qk = _silu(y)
        if cblk < M_WIDTH // MIX_TN:
            qm_ref[0, :, cols] = (qk * (M_HEAD_DIM ** -0.5)).astype(BF16)
        else:
            km_ref[0, :, cblk * MIX_TN - M_WIDTH:(cblk + 1) * MIX_TN - M_WIDTH] = qk.astype(BF16)

        qa = _bdot(cq, wuq_ref[:, cols])
        kn = _bdot(ckv, wk_ref[:, cols])
        for hh in range(2):
            hcols = slice((2 * cblk + hh) * LANES, (2 * cblk + hh + 1) * LANES)
            blk = qa[:, hh * LANES:(hh + 1) * LANES]
            rot = blk * cosv + _rope_partner(blk) * sinv
            qa_ref[0, :, hcols] = (rot * scale).astype(BF16)
            ka_ref[0, :, hcols] = (kn[:, hh * LANES:(hh + 1) * LANES] + krot).astype(BF16)

        vo = _bdot(u, w_ref[:, 2 * M_WIDTH + cblk * MIX_TN:2 * M_WIDTH + (cblk + 1) * MIX_TN])
        if cblk < M_WIDTH // MIX_TN:
            vm_ref[0, :, cols] = vo.astype(BF16)
        else:
            om_ref[0, :, cblk * MIX_TN - M_WIDTH:(cblk + 1) * MIX_TN - M_WIDTH] = vo

    gpre = _dot_nt(wgt_ref[...], u) + bcol_ref[...]
    row = lax.broadcasted_iota(jnp.int32, gpre.shape, 0)
    gates = jnp.where(row < M_HEADS, gpre, _log_sigmoid(gpre))
    for cix in range(tm // M_CHUNK):
        gr_ref[0, 2 * M_HEADS * cix:2 * M_HEADS * (cix + 1), :] = gates[:, cix * M_CHUNK:(cix + 1) * M_CHUNK]


def _mix_in(x, ada8, pos3, g_mix, w_b, w_gt, conv_w, conv_b, b_col, g_q, g_kv, wuq, wk, wv,
            invf, sgn):
    B, S, D = x.shape
    tm = min(MIX_TM, S)
    nt = S // tm
    hb = tm // HALO
    const = lambda shape: pl.BlockSpec(shape, lambda b, i: (0,) * len(shape))
    tok = lambda w: pl.BlockSpec((1, tm, w), lambda b, i: (b, i, 0))
    out_shapes = (
        jax.ShapeDtypeStruct((B, S, M_WIDTH), BF16),
        jax.ShapeDtypeStruct((B, S, M_WIDTH), BF16),
        jax.ShapeDtypeStruct((B, S, M_WIDTH), BF16),
        jax.ShapeDtypeStruct((B, S, M_WIDTH), F32),
        jax.ShapeDtypeStruct((B, 2 * M_HEADS * (S // M_CHUNK), M_CHUNK), F32),
        jax.ShapeDtypeStruct((B, S, A_HEADS * LANES), BF16),
        jax.ShapeDtypeStruct((B, S, A_HEADS * LANES), BF16),
        jax.ShapeDtypeStruct((B, S, A_WIDTH), BF16),
    )
    return pl.pallas_call(
        _mix_in_kernel,
        grid=(B, nt),
        in_specs=[tok(D),
                  pl.BlockSpec((1, HALO, D), lambda b, i: (b, jnp.maximum(i * hb - 1, 0), 0)),
                  pl.BlockSpec((1,) + ada8.shape[1:], lambda b, i: (b, 0, 0)),
                  pl.BlockSpec((1, tm // LANES, LANES), lambda b, i: (b, i, 0)),
                  const(g_mix.shape), const(w_b.shape), const(w_gt.shape), const(conv_w.shape),
                  const(conv_b.shape), const(b_col.shape), const(g_q.shape), const(g_kv.shape),
                  const(wuq.shape), const(wk.shape), const(wv.shape), const(invf.shape),
                  const(sgn.shape)],
        out_specs=(tok(M_WIDTH), tok(M_WIDTH), tok(M_WIDTH), tok(M_WIDTH),
                   pl.BlockSpec((1, 2 * M_HEADS * (tm // M_CHUNK), M_CHUNK), lambda b, i: (b, i, 0)),
                   tok(A_HEADS * LANES), tok(A_HEADS * LANES), tok(A_WIDTH)),
        out_shape=out_shapes,
        scratch_shapes=[pltpu.VMEM((HALO + tm, 2 * M_WIDTH), F32)],
        compiler_params=pltpu.CompilerParams(dimension_semantics=("parallel", "arbitrary"),
                                             vmem_limit_bytes=VMEM_LIMIT),
        name="mix_in",
    )(x, x, ada8, pos3, g_mix, w_b, w_gt, conv_w, conv_b, b_col, g_q, g_kv, wuq, wk, wv, invf, sgn)


def _aligned(i, m):
    return i if isinstance(i, int) else pl.multiple_of(i, m)


def _split3(x):
    hi = x.astype(BF16)
    r = x - hi.astype(F32)
    mid = r.astype(BF16)
    lo = (r - mid.astype(F32)).astype(BF16)
    return hi, mid, lo


def _mlstm_kernel(q_ref, k_ref, v_ref, om_ref, gr_ref, gmh_ref, uo_ref, o_ref,
                  c_scr, rows_scr, dw_scr, iw_scr, em_scr, wc_scr):
    S = q_ref.shape[1]
    L = M_CHUNK
    dh = M_HEAD_DIM
    H = M_HEADS
    nc = S // L
    R = 2 * H * nc

    g = gr_ref[0] * LOG2E
    prod = _bdot(jnp.concatenate(_split3(g), axis=0), uo_ref[...])
    prod = prod[:R] + prod[R:2 * R] + prod[2 * R:]
    cs = prod[:, :L]
    tot = prod[:, L:]
    li = pltpu.roll(g, H, 0)
    wl = (tot - cs) + li
    wmax = jnp.broadcast_to(jnp.max(wl, axis=-1, keepdims=True), (R, L))
    m_cur = jnp.zeros((2 * H, L), F32)
    m_prev, m_new = [], []
    for c in range(nc):
        rs = slice(2 * H * c, 2 * H * (c + 1))
        m_prev.append(m_cur)
        m_cur = jnp.maximum(tot[rs] + m_cur, wmax[rs])
        m_new.append(m_cur)
    m_prev = jnp.concatenate(m_prev, axis=0)
    m_new = jnp.concatenate(m_new, axis=0)
    rows_scr[0] = cs
    rows_scr[1] = li
    rows_scr[2] = wl
    rows_scr[3] = m_prev
    rows_scr[4] = m_new
    rows_scr[5] = jnp.exp2(tot + m_prev - m_new)

    causal = (lax.broadcasted_iota(jnp.int32, (L, L), 0) >= lax.broadcasted_iota(jnp.int32, (L, L), 1))
    upper = lax.broadcasted_iota(jnp.int32, (2 * H, L), 0) < H

    def prep(c, slot):
        rs = pl.ds(_aligned(c * 2 * H, 2 * H), 2 * H)
        cs_c, li_c, wl_c = rows_scr[0, rs, :], rows_scr[1, rs, :], rows_scr[2, rs, :]
        mp_c, mn_c = rows_scr[3, rs, :], rows_scr[4, rs, :]
        z = jnp.where(upper, pltpu.roll(wl_c, H, 0), cs_c)
        cols = jnp.concatenate([z] * (L // (2 * H)), axis=0).T
        for h in range(H):
            wl_col = jnp.broadcast_to(cols[:, h:h + 1], (L, L))
            b_col = jnp.broadcast_to(cols[:, H + h:H + h + 1], (L, L))
            r = H + h
            dlog = jnp.where(causal, (b_col - cs_c[r:r + 1, :]) + li_c[r:r + 1, :], -jnp.inf)
            inter_log = b_col + mp_c[r:r + 1, :]
            m_t = jnp.maximum(inter_log, jnp.max(dlog, axis=-1, keepdims=True))
            dw_scr[slot, h] = jnp.exp2(dlog - m_t)
            iw_scr[slot, h] = jnp.exp2(inter_log - m_t)
            em_scr[slot, h] = jnp.exp2(-m_t)
            wc_scr[slot, h] = jnp.exp2(wl_col - mn_c[r:r + 1, :])

    c_scr[...] = jnp.zeros_like(c_scr)
    ones = jnp.ones((L, dh), BF16)

    def chunk(c, slot):
        ts = pl.ds(_aligned(c * L, L), L)
        for h in range(H):
            hs = slice(h * dh, (h + 1) * dh)
            q = q_ref[0, ts, hs]
            k = k_ref[0, ts, hs]
            v_aug = jnp.concatenate([v_ref[0, ts, hs], ones], axis=1)
            c_aug = c_scr[h]
            iw = iw_scr[slot, h]
            s = _dot_nt(q, k) * dw_scr[slot, h]
            qc = _bdot(q, c_aug.astype(BF16))
            sv = _bdot(s.astype(BF16), v_aug)
            num = sv[:, :dh] + iw * qc[:, :dh]
            den = sv[:, dh:] + iw * qc[:, dh:]
            hv = num / jnp.maximum(jnp.abs(den), em_scr[slot, h])
            hn = _rms(hv, gmh_ref[:, hs])
            o_ref[0, ts, hs] = (_sigmoid(om_ref[0, ts, hs]) * hn).astype(o_ref.dtype)

            decay = rows_scr[5, pl.ds(c * 2 * H + H + h, 1), :]
            kw_t = (k.astype(F32) * wc_scr[slot, h]).T.astype(BF16)
            c_scr[h] = jnp.concatenate([decay, decay], axis=1) * c_aug + _bdot(kw_t, v_aug)

    unroll = M_UNROLL if nc % M_UNROLL == 0 else 2
    prep(0, 0)
    if unroll == nc:
        for c in range(nc):
            chunk(c, c % 2)
            if c + 1 < nc:
                prep(c + 1, (c + 1) % 2)
        return

    def body(i, carry):
        c = unroll * i
        for k in range(unroll):
            chunk(c + k, k % 2)
            prep(jnp.minimum(c + k + 1, nc - 1), (k + 1) % 2)
        return carry

    lax.fori_loop(0, nc // unroll, body, 0)


def _mlstm(qm, km, vm, om, gr, g_mhead):
    B, S, W = qm.shape
    L = M_CHUNK
    uo = np.concatenate([np.triu(np.ones((L, L), np.float32)), np.ones((L, L), np.float32)], axis=1)
    seq = lambda: pl.BlockSpec((1, S, W), lambda b: (b, 0, 0))
    rows = 2 * M_HEADS * (S // L)
    assert (S // L) % 2 == 0, "the chunk loop handles two chunks per iteration"
    per_head = lambda: pltpu.VMEM((2, M_HEADS, L, L), F32)
    return pl.pallas_call(
        _mlstm_kernel,
        grid=(B,),
        in_specs=[seq(), seq(), seq(), seq(),
                  pl.BlockSpec((1, rows, L), lambda b: (b, 0, 0)),
                  pl.BlockSpec((SUBLANES, W), lambda b: (0, 0)),
                  pl.BlockSpec((L, 2 * L), lambda b: (0, 0))],
        out_specs=seq(),
        out_shape=jax.ShapeDtypeStruct((B, S, W), BF16),
        scratch_shapes=[pltpu.VMEM((M_HEADS, M_HEAD_DIM, 2 * M_HEAD_DIM), F32),
                        pltpu.VMEM((6, rows, L), F32),
                        per_head(), per_head(), per_head(), per_head()],
        compiler_params=pltpu.CompilerParams(dimension_semantics=("parallel",),
                                             vmem_limit_bytes=VMEM_LIMIT),
        name="mlstm",
    )(qm, km, vm, om, gr, g_mhead, jnp.asarray(uo, BF16))


def _attn_kernel(q_ref, k_ref, v_ref, o_ref, m_scr, acc_scr, s_scr, p_scr, al_scr):
    S = q_ref.shape[1]
    t = min(ATT_T, S)
    nq = S // t
    nh = ATT_HEADS
    causal = (lax.broadcasted_iota(jnp.int32, (t, t), 1)
              <= lax.broadcasted_iota(jnp.int32, (t, t), 0))

    def tile(i):
        return slice(i * t, (i + 1) * t)

    def head(hh):
        return slice(hh * LANES, (hh + 1) * LANES)

    def scores(blk, slot):
        qi, kt = blk
        for hh in range(nh):
            s_scr[slot, hh] = _dot_nt(q_ref[0, tile(qi), head(hh)], k_ref[0, tile(kt), head(hh)])

    def softmax(blk, slot, diag):
        qi, _ = blk
        for hh in range(nh):
            mask = (lambda x: jnp.where(causal, x, -jnp.inf)) if diag else (lambda x: x)
            row_max = jnp.max(mask(s_scr[slot, hh]), axis=-1, keepdims=True)
            if diag:
                m_new = jnp.broadcast_to(row_max, (t, LANES))
            else:
                m_old = m_scr[hh, tile(qi), :]
                m_new = jnp.maximum(m_old, row_max)
                al_scr[slot, hh] = jnp.exp2(m_old - m_new)
            m_scr[hh, tile(qi), :] = m_new
            p = jnp.exp2(mask(s_scr[slot, hh]) - jnp.concatenate([m_new] * (t // LANES), axis=1))
            p_scr[slot, hh // 2, (hh % 2) * t:(hh % 2 + 1) * t, :] = p.astype(BF16)

    ones = jnp.ones((t, LANES), BF16)

    def values(blk, slot, diag):
        qi, kt = blk
        for hh in range(nh):
            if hh % 2 == 0:
                pv = _bdot(p_scr[slot, hh // 2],
                           jnp.concatenate([v_ref[0, tile(kt), head(hh // 2)], ones], axis=1))
            part = pv[(hh % 2) * t:(hh % 2 + 1) * t]
            if diag:
                acc_scr[hh, tile(qi), :] = part
            else:
                alpha = al_scr[slot, hh]
                acc_scr[hh, tile(qi), :] = (jnp.concatenate([alpha, alpha], axis=1)
                                            * acc_scr[hh, tile(qi), :] + part)

    blocks = [(j, j) for j in range(nq)] + [(qi, kt) for qi in range(1, nq) for kt in range(qi)]
    n = len(blocks)
    for j in range(n + 2):
        slot = j % 2
        if j >= 2:
            values(blocks[j - 2], slot, j - 2 < nq)
        if 1 <= j <= n:
            softmax(blocks[j - 1], 1 - slot, j - 1 < nq)
        if j < n:
            scores(blocks[j], slot)

    lane = lax.broadcasted_iota(jnp.int32, (t, LANES), 1)
    for qi in range(nq):
        for pair in range(nh // 2):
            outs = [acc_scr[hh, tile(qi), :LANES] / acc_scr[hh, tile(qi), LANES:]
                    for hh in (2 * pair, 2 * pair + 1)]
            o_ref[0, tile(qi), head(pair)] = jnp.where(lane < A_V, outs[0], outs[1]).astype(o_ref.dtype)


def _attn(qa, ka, va):
    B, S, _ = qa.shape
    t = min(ATT_T, S)
    nh = ATT_HEADS
    group = lambda w: pl.BlockSpec((1, S, w), lambda b, p: (b, 0, p))
    return pl.pallas_call(
        _attn_kernel,
        grid=(B, A_HEADS // nh),
        in_specs=[group(nh * LANES), group(nh * LANES), group(nh * A_V)],
        out_specs=group(nh * A_V),
        out_shape=jax.ShapeDtypeStruct((B, S, A_WIDTH), BF16),
        scratch_shapes=[pltpu.VMEM((nh, S, LANES), F32), pltpu.VMEM((nh, S, 2 * LANES), F32),
                        pltpu.VMEM((2, nh, t, t), F32), pltpu.VMEM((2, nh // 2, 2 * t, t), BF16),
                        pltpu.VMEM((2, nh, t, LANES), F32)],
        compiler_params=pltpu.CompilerParams(dimension_semantics=("parallel", "parallel"),
                                             vmem_limit_bytes=VMEM_LIMIT),
        name="attn",
    )(qa, ka, va)


def _merge_kernel(x_ref, ada_ref, hm_ref, oa_ref, gmix_ref, wg_ref, wbm_ref, wba_ref, wo_ref, h_ref):
    x = x_ref[0]
    tm, D = x.shape
    gain = gmix_ref[...] * (1.0 + _ada_rows(ada_ref, 1))
    u = (_rms(x, gain) + _rows(_ada_rows(ada_ref, 0), tm)).astype(BF16)
    y = _sigmoid(_bdot(u, wg_ref[:, 0:D])) * _bdot(hm_ref[0], wbm_ref[...])
    y = y + _sigmoid(_bdot(u, wg_ref[:, D:2 * D])) * _bdot(oa_ref[0], wba_ref[...])
    h_ref[0] = x + _rows(_ada_rows(ada_ref, 2), tm) * _bdot(y.astype(BF16), wo_ref[...])


def _merge(x, ada8, hm, oa, g_mix, w_g, w_bm, w_ba, w_o):
    B, S, D = x.shape
    tm = min(MERGE_TM, S)
    const = lambda shape: pl.BlockSpec(shape, lambda b, i: (0,) * len(shape),
                                       pipeline_mode=pl.Buffered(1))
    tok = lambda w: pl.BlockSpec((1, tm, w), lambda b, i: (b, i, 0))
    return pl.pallas_call(
        _merge_kernel,
        grid=(B, S // tm),
        in_specs=[tok(D), pl.BlockSpec((1,) + ada8.shape[1:], lambda b, i: (b, 0, 0)), tok(M_WIDTH), tok(A_WIDTH),
                  const(g_mix.shape), const(w_g.shape), const(w_bm.shape), const(w_ba.shape),
                  const(w_o.shape)],
        out_specs=tok(D),
        out_shape=jax.ShapeDtypeStruct((B, S, D), F32),
        compiler_params=pltpu.CompilerParams(dimension_semantics=("parallel", "parallel"),
                                             vmem_limit_bytes=VMEM_LIMIT),
        name="merge",
    )(x, ada8, hm, oa, g_mix, w_g, w_bm, w_ba, w_o)


def _ffn_kernel(h_ref, ada_ref, gffn_ref, gfin_ref, wi_ref, wo_ref, o_ref):
    h = h_ref[0]
    tm = h.shape[0]
    gain = gffn_ref[...] * (1.0 + _ada_rows(ada_ref, 4))
    u = (_rms(h, gain) + _rows(_ada_rows(ada_ref, 3), tm)).astype(BF16)
    acc = None
    for lo, hi in FFN_CHUNKS:
        a = _silu(_bdot(u, wi_ref[:, lo:hi])) * _bdot(u, wi_ref[:, D_FF + lo:D_FF + hi])
        part = _bdot(a.astype(BF16), wo_ref[lo:hi, :])
        acc = part if acc is None else acc + part
    o_ref[0] = _rms(h + _rows(_ada_rows(ada_ref, 5), tm) * acc, gfin_ref[...])


def _ffn(h1, ada8, g_ffn, g_final, w_in, w_out):
    B, S, D = h1.shape
    tm = min(FFN_TM, S)
    tok = pl.BlockSpec((1, tm, D), lambda b, i: (b, i, 0))
    resident = lambda shape: pl.BlockSpec(shape, lambda b, i: (0, 0), pipeline_mode=pl.Buffered(1))
    return pl.pallas_call(
        _ffn_kernel,
        grid=(B, S // tm),
        in_specs=[tok,
                  pl.BlockSpec((1,) + ada8.shape[1:], lambda b, i: (b, 0, 0)),
                  resident(g_ffn.shape), resident(g_final.shape), resident(w_in.shape),
                  resident(w_out.shape)],
        out_specs=tok,
        out_shape=jax.ShapeDtypeStruct((B, S, D), F32),
        compiler_params=pltpu.CompilerParams(dimension_semantics=("parallel", "parallel"),
                                             vmem_limit_bytes=VMEM_LIMIT),
        name="ffn",
    )(h1, ada8, g_ffn, g_final, w_in, w_out)


def _pack_weights(w_in, w_uq, w_ukv):
    D = w_in.shape[0]
    o_gate = 4 * M_WIDTH
    o_qlat = o_gate + 2 * M_HEADS
    o_kr = o_qlat + A_Q_RANK + A_KV_RANK
    o_gm = o_kr + A_ROPE
    zeros = lambda n: jnp.zeros((D, n), w_in.dtype)
    half = A_ROPE // 2
    misc = jnp.concatenate([zeros(A_NOPE), w_in[:, o_kr:o_gm], w_in[:, o_kr:o_kr + half],
                            zeros(LANES - A_QK - half)], axis=1)
    w_b = jnp.concatenate([w_in[:, :o_gate], w_in[:, o_qlat:o_kr], misc], axis=1).astype(BF16)
    w_gt = w_in[:, o_gate:o_qlat].T.astype(BF16)
    w_g = w_in[:, o_gm:].astype(BF16)
    wuq = w_uq.reshape(A_Q_RANK, A_HEADS, A_QK)
    wuq = jnp.concatenate([wuq, wuq[:, :, A_NOPE:A_NOPE + half],
                           jnp.zeros((A_Q_RANK, A_HEADS, LANES - A_QK - half), w_uq.dtype)], axis=2)
    wuq = wuq.reshape(A_Q_RANK, A_HEADS * LANES).astype(BF16)
    wkv = w_ukv.reshape(A_KV_RANK, A_HEADS, A_NOPE + A_V)
    wk = jnp.pad(wkv[:, :, :A_NOPE], ((0, 0), (0, 0), (0, LANES - A_NOPE)))
    wk = wk.reshape(A_KV_RANK, A_HEADS * LANES).astype(BF16)
    wv = wkv[:, :, A_NOPE:].reshape(A_KV_RANK, A_WIDTH).astype(BF16)
    return w_b, w_gt, w_g, wuq, wk, wv


def _rope_rows(groups):
    inv_freq = ROPE_THETA ** (-jnp.arange(0, A_ROPE, 2, dtype=F32) / A_ROPE)
    half = A_ROPE // 2
    pair = jnp.concatenate([inv_freq, inv_freq])
    rows = []
    for g in range(groups):
        row = jnp.zeros((LANES,), F32).at[A_ROPE * g:A_ROPE * (g + 1)].set(pair)
        rows.append(jnp.broadcast_to(row, (SUBLANES, LANES)))
    sign = jnp.concatenate([-jnp.ones((half,), F32), jnp.ones((half,), F32)])
    sgn = jnp.broadcast_to(jnp.tile(sign, LANES // A_ROPE), (SUBLANES, LANES))
    return jnp.concatenate(rows, axis=0), sgn


def _rep8(v):
    v = v.reshape(-1, v.shape[-1])
    return jnp.repeat(v, SUBLANES, axis=0)


def kernel(x, c, positions, w_ada, b_ada, g_mix, w_in, conv_w, conv_b, b_igate, b_fgate, g_mhead,
           g_q_lat, w_uq, g_kv_lat, w_ukv, w_branch_m, w_branch_a, w_out, g_ffn, w_ffn_in,
           w_ffn_out, g_final):
    B, S, D = x.shape
    assert w_ada.shape[0] == 1, "the final rmsnorm is fused into the (single) layer's FFN kernel"
    invf, sgn = _rope_rows(LANES // A_ROPE)
    pos3 = positions.reshape(B, S // LANES, LANES)
    l = 0
    ada = _ada(c, w_ada[l], b_ada[l])
    ada8 = jnp.repeat(ada.reshape(B, 6, 1, D), SUBLANES, axis=2).reshape(B, 6 * SUBLANES, D)
    w_b, w_gt, w_g, wuq, wk, wv = _pack_weights(w_in[l], w_uq[l], w_ukv[l])
    b_col = jnp.concatenate([b_igate[l], b_fgate[l]]).astype(F32).reshape(2 * M_HEADS, 1)
    gmix = _rep8(g_mix[l])
    qm, km, vm, om, gr, qa, ka, va = _mix_in(
        x, ada8, pos3, gmix, w_b, w_gt, _rep8(conv_w[l]), _rep8(conv_b[l]), b_col,
        _rep8(g_q_lat[l]), _rep8(g_kv_lat[l]), wuq, wk, wv, invf, sgn)
    hm = _mlstm(qm, km, vm, om, gr, _rep8(g_mhead[l]))
    oa = _attn(qa, ka, va)
    h1 = _merge(x, ada8, hm, oa, gmix, w_g, w_branch_m[l].astype(BF16),
                w_branch_a[l].astype(BF16), w_out[l].astype(BF16))
    return _ffn(h1, ada8, _rep8(g_ffn[l]), _rep8(g_final), w_ffn_in[l].astype(BF16),
                w_ffn_out[l].astype(BF16))
```

```python
import functools

import jax
import jax.numpy as jnp
import numpy as np
from jax import lax
from jax.experimental import pallas as pl
from jax.experimental.pallas import tpu as pltpu

F32 = jnp.float32
BF16 = jnp.bfloat16

D_MODEL = 1024
M_HEADS = 4
M_HEAD_DIM = 128
M_WIDTH = M_HEADS * M_HEAD_DIM
M_CONV = 4
A_HEADS = 8
A_NOPE = 64
A_ROPE = 32
A_V = 64
A_QK = A_NOPE + A_ROPE
A_Q_RANK = 256
A_KV_RANK = 128
A_WIDTH = A_HEADS * A_V
ROPE_THETA = 10000.0
D_FF = 2816
NORM_EPS = 1e-6

LANES = 128
SUBLANES = 8
HALO = 8
VMEM_LIMIT = 56 * 1024 * 1024

MIX_TM = 1024
MERGE_TM = 1024
MIX_TN = 256
M_CHUNK = 128
M_UNROLL = 2
ATT_T = 256
ATT_HEADS = 2
LOG2E = 1.4426950408889634
FFN_TM = 1024
MXU_N = 256
FFN_CHUNKS = ((0, 6 * MXU_N), (6 * MXU_N, D_FF))


def _sigmoid(x):
    return 0.5 * jnp.tanh(0.5 * x) + 0.5


def _silu(x):
    h = 0.5 * x
    return h + h * jnp.tanh(h)


def _log_sigmoid(x):
    return jnp.minimum(x, 0.0) - jnp.log1p(jnp.exp(-jnp.abs(x)))


def _rows(v8, n):
    return v8 if n == SUBLANES else jnp.concatenate([v8] * (n // SUBLANES), axis=0)


def _rms(x, g8):
    n, d = x.shape
    ms = jnp.broadcast_to(jnp.sum(x * x, axis=-1, keepdims=True), (n, LANES)) * (1.0 / d)
    r = lax.rsqrt(ms + NORM_EPS)
    if d > LANES:
        r = jnp.concatenate([r] * (d // LANES), axis=1)
    return x * r * _rows(g8, n)


def _ada_rows(ada_ref, k):
    return ada_ref[0, SUBLANES * k:SUBLANES * (k + 1), :]


def _bdot(a, b):
    return jnp.dot(a, b, preferred_element_type=F32)


def _dot_nt(a, b):
    return lax.dot_general(a, b, (((1,), (1,)), ((), ())), preferred_element_type=F32)


def _ada_kernel(c_ref, w_ref, b_ref, o_ref):
    c = c_ref[...]
    o_ref[...] = jnp.dot(_silu(c), w_ref[...], preferred_element_type=F32,
                         precision=lax.Precision.HIGHEST) + b_ref[...]


def _ada(c, w_ada, b_ada):
    B, D = c.shape
    N = w_ada.shape[1]
    return pl.pallas_call(
        _ada_kernel,
        grid=(N // D,),
        in_specs=[pl.BlockSpec((B, D), lambda j: (0, 0)),
                  pl.BlockSpec((D, D), lambda j: (0, j)),
                  pl.BlockSpec((1, D), lambda j: (0, j))],
        out_specs=pl.BlockSpec((B, D), lambda j: (0, j)),
        out_shape=jax.ShapeDtypeStruct((B, N), F32),
        compiler_params=pltpu.CompilerParams(dimension_semantics=("arbitrary",),
                                             vmem_limit_bytes=VMEM_LIMIT),
        name="ada",
    )(c, w_ada, b_ada.reshape(1, N))


def _rope_partner(blk):
    return pltpu.roll(blk, LANES - A_ROPE // 2, 1)


def _mix_in_kernel(x_ref, xh_ref, ada_ref, pos_ref, gmix_ref, w_ref, wgt_ref, cw_ref, cb_ref,
                   bcol_ref, gq_ref, gkv_ref, wuq_ref, wk_ref, wv_ref, invf_ref, sgn_ref,
                   qm_ref, km_ref, vm_ref, om_ref, gr_ref, qa_ref, ka_ref, va_ref, z_scr):
    i = pl.program_id(1)
    tm = x_ref.shape[1]
    sh1 = _ada_rows(ada_ref, 0)
    gain = gmix_ref[...] * (1.0 + _ada_rows(ada_ref, 1))

    def modulate(xv):
        return _rms(xv, gain) + _rows(sh1, xv.shape[0])

    u = modulate(x_ref[0]).astype(BF16)
    uh = modulate(xh_ref[0]).astype(BF16)

    lat = _bdot(u, w_ref[:, 4 * M_WIDTH:4 * M_WIDTH + 4 * LANES])
    q_lat = lat[:, 0:A_Q_RANK]
    kv_lat = lat[:, A_Q_RANK:A_Q_RANK + A_KV_RANK]
    misc = lat[:, A_Q_RANK + A_KV_RANK:]

    pos = pos_ref[0].astype(F32)
    lane_groups = LANES // A_ROPE
    lane_g = lax.broadcasted_iota(jnp.int32, (LANES, LANES), 1)
    rope_g = (lane_g >= A_NOPE) & (lane_g < A_QK)
    cos_parts, sin_parts = [], []
    for pack in range(tm // (LANES * lane_groups)):
        ang = None
        for lg in range(lane_groups):
            gi = pack * lane_groups + lg
            by_row = jnp.broadcast_to(pos[gi:gi + 1, :], (LANES, LANES)).T
            term = by_row * _rows(invf_ref[SUBLANES * lg:SUBLANES * (lg + 1), :], LANES)
            ang = term if ang is None else ang + term
        cos_p = jnp.cos(ang)
        sin_p = jnp.sin(ang) * _rows(sgn_ref[...], LANES)
        for lg in range(lane_groups):
            shift = (A_NOPE - A_ROPE * lg) % LANES
            cg = pltpu.roll(cos_p, shift, 1) if shift else cos_p
            sg = pltpu.roll(sin_p, shift, 1) if shift else sin_p
            cos_parts.append(jnp.where(rope_g, cg, 1.0))
            sin_parts.append(jnp.where(rope_g, sg, 0.0))
    cosv = jnp.concatenate(cos_parts, axis=0)
    sinv = jnp.concatenate(sin_parts, axis=0)
    lane = lax.broadcasted_iota(jnp.int32, (tm, LANES), 1)
    is_rope = (lane >= A_NOPE) & (lane < A_QK)

    cq = _rms(q_lat, gq_ref[...]).astype(BF16)
    ckv = _rms(kv_lat, gkv_ref[...]).astype(BF16)
    va_ref[0] = _bdot(ckv, wv_ref[...]).astype(BF16)
    krot = jnp.where(is_rope, misc * cosv + _rope_partner(misc) * sinv, 0.0)
    scale = A_QK ** -0.5 * LOG2E
    cosq = cosv * scale
    sinq = sinv * scale

    assert 2 * M_WIDTH // MIX_TN == A_HEADS // 2
    for cblk in range(2 * M_WIDTH // MIX_TN):
        cols = slice(cblk * MIX_TN, (cblk + 1) * MIX_TN)
        z_scr[0:HALO, cols] = jnp.where(i > 0, _bdot(uh, w_ref[:, cols]), 0.0)
        z_scr[HALO:HALO + tm, cols] = _bdot(u, w_ref[:, cols])
        y = _rows(cb_ref[:, cols], tm)
        for j in range(M_CONV):
            sft = M_CONV - 1 - j
            y = y + (z_scr[HALO - sft:HALO - sft + tm, cols]
                     * _rows(cw_ref[SUBLANES * j:SUBLANES * (j + 1), cols], tm))
        qk = _silu(y)
        if cblk < M_WIDTH // MIX_TN:
            qm_ref[0, :, cols] = (qk * (M_HEAD_DIM ** -0.5)).astype(BF16)
        else:
            km_ref[0, :, cblk * MIX_TN - M_WIDTH:(cblk + 1) * MIX_TN - M_WIDTH] = qk.astype(BF16)

        qa = _bdot(cq, wuq_ref[:, cols])
        kn = _bdot(ckv, wk_ref[:, cols])
        for hh in range(2):
            hcols = slice((2 * cblk + hh) * LANES, (2 * cblk + hh + 1) * LANES)
            blk = qa[:, hh * LANES:(hh + 1) * LANES]
            qa_ref[0, :, hcols] = (blk * cosq + _rope_partner(blk) * sinq).astype(BF16)
            ka_ref[0, :, hcols] = (kn[:, hh * LANES:(hh + 1) * LANES] + krot).astype(BF16)

        vo = _bdot(u, w_ref[:, 2 * M_WIDTH + cblk * MIX_TN:2 * M_WIDTH + (cblk + 1) * MIX_TN])
        if cblk < M_WIDTH // MIX_TN:
            vm_ref[0, :, cols] = vo.astype(BF16)
        else:
            om_ref[0, :, cblk * MIX_TN - M_WIDTH:(cblk + 1) * MIX_TN - M_WIDTH] = vo

    gpre = _dot_nt(wgt_ref[...], u) + bcol_ref[...]
    row = lax.broadcasted_iota(jnp.int32, gpre.shape, 0)
    gates = jnp.where(row < M_HEADS, gpre, _log_sigmoid(gpre))
    for cix in range(tm // M_CHUNK):
        gr_ref[0, 2 * M_HEADS * cix:2 * M_HEADS * (cix + 1), :] = gates[:, cix * M_CHUNK:(cix + 1) * M_CHUNK]


def _mix_in(x, ada8, pos3, g_mix, w_b, w_gt, conv_w, conv_b, b_col, g_q, g_kv, wuq, wk, wv,
            invf, sgn):
    B, S, D = x.shape
    tm = min(MIX_TM, S)
    nt = S // tm
    hb = tm // HALO
    const = lambda shape: pl.BlockSpec(shape, lambda b, i: (0,) * len(shape))
    tok = lambda w: pl.BlockSpec((1, tm, w), lambda b, i: (b, i, 0))
    out_shapes = (
        jax.ShapeDtypeStruct((B, S, M_WIDTH), BF16),
        jax.ShapeDtypeStruct((B, S, M_WIDTH), BF16),
        jax.ShapeDtypeStruct((B, S, M_WIDTH), BF16),
        jax.ShapeDtypeStruct((B, S, M_WIDTH), F32),
        jax.ShapeDtypeStruct((B, 2 * M_HEADS * (S // M_CHUNK), M_CHUNK), F32),
        jax.ShapeDtypeStruct((B, S, A_HEADS * LANES), BF16),
        jax.ShapeDtypeStruct((B, S, A_HEADS * LANES), BF16),
        jax.ShapeDtypeStruct((B, S, A_WIDTH), BF16),
    )
    return pl.pallas_call(
        _mix_in_kernel,
        grid=(B, nt),
        in_specs=[tok(D),
                  pl.BlockSpec((1, HALO, D), lambda b, i: (b, jnp.maximum(i * hb - 1, 0), 0)),
                  pl.BlockSpec((1,) + ada8.shape[1:], lambda b, i: (b, 0, 0)),
                  pl.BlockSpec((1, tm // LANES, LANES), lambda b, i: (b, i, 0)),
                  const(g_mix.shape), const(w_b.shape), const(w_gt.shape), const(conv_w.shape),
                  const(conv_b.shape), const(b_col.shape), const(g_q.shape), const(g_kv.shape),
                  const(wuq.shape), const(wk.shape), const(wv.shape), const(invf.shape),
                  const(sgn.shape)],
        out_specs=(tok(M_WIDTH), tok(M_WIDTH), tok(M_WIDTH), tok(M_WIDTH),
                   pl.BlockSpec((1, 2 * M_HEADS * (tm // M_CHUNK), M_CHUNK), lambda b, i: (b, i, 0)),
                   tok(A_HEADS * LANES), tok(A_HEADS * LANES), tok(A_WIDTH)),
        out_shape=out_shapes,
        scratch_shapes=[pltpu.VMEM((HALO + tm, 2 * M_WIDTH), F32)],
        compiler_params=pltpu.CompilerParams(dimension_semantics=("parallel", "arbitrary"),
                                             vmem_limit_bytes=VMEM_LIMIT),
        name="mix_in",
    )(x, x, ada8, pos3, g_mix, w_b, w_gt, conv_w, conv_b, b_col, g_q, g_kv, wuq, wk, wv, invf, sgn)


def _aligned(i, m):
    return i if isinstance(i, int) else pl.multiple_of(i, m)


def _split3(x):
    hi = x.astype(BF16)
    r = x - hi.astype(F32)
    mid = r.astype(BF16)
    lo = (r - mid.astype(F32)).astype(BF16)
    return hi, mid, lo


def _mlstm_kernel(q_ref, k_ref, v_ref, om_ref, gr_ref, gmh_ref, uo_ref, o_ref,
                  c_scr, rows_scr, dw_scr, iw_scr, em_scr, wc_scr):
    S = q_ref.shape[1]
    L = M_CHUNK
    dh = M_HEAD_DIM
    H = M_HEADS
    nc = S // L
    R = 2 * H * nc

    g = gr_ref[0] * LOG2E
    prod = _bdot(jnp.concatenate(_split3(g), axis=0), uo_ref[...])
    prod = prod[:R] + prod[R:2 * R] + prod[2 * R:]
    cs = prod[:, :L]
    tot = prod[:, L:]
    li = pltpu.roll(g, H, 0)
    wl = (tot - cs) + li
    wmax = jnp.broadcast_to(jnp.max(wl, axis=-1, keepdims=True), (R, L))
    m_cur = jnp.zeros((2 * H, L), F32)
    m_prev, m_new = [], []
    for c in range(nc):
        rs = slice(2 * H * c, 2 * H * (c + 1))
        m_prev.append(m_cur)
        m_cur = jnp.maximum(tot[rs] + m_cur, wmax[rs])
        m_new.append(m_cur)
    m_prev = jnp.concatenate(m_prev, axis=0)
    m_new = jnp.concatenate(m_new, axis=0)
    rows_scr[0] = cs
    rows_scr[1] = li
    rows_scr[2] = wl
    rows_scr[3] = m_prev
    rows_scr[4] = m_new
    rows_scr[5] = jnp.exp2(tot + m_prev - m_new)

    causal = (lax.broadcasted_iota(jnp.int32, (L, L), 0) >= lax.broadcasted_iota(jnp.int32, (L, L), 1))
    upper = lax.broadcasted_iota(jnp.int32, (2 * H, L), 0) < H

    def prep(c, slot):
        rs = pl.ds(_aligned(c * 2 * H, 2 * H), 2 * H)
        cs_c, li_c, wl_c = rows_scr[0, rs, :], rows_scr[1, rs, :], rows_scr[2, rs, :]
        mp_c, mn_c = rows_scr[3, rs, :], rows_scr[4, rs, :]
        z = jnp.where(upper, pltpu.roll(wl_c, H, 0), cs_c)
        cols = jnp.concatenate([z] * (L // (2 * H)), axis=0).T
        for h in range(H):
            wl_col = jnp.broadcast_to(cols[:, h:h + 1], (L, L))
            b_col = jnp.broadcast_to(cols[:, H + h:H + h + 1], (L, L))
            r = H + h
            dlog = jnp.where(causal, (b_col - cs_c[r:r + 1, :]) + li_c[r:r + 1, :], -jnp.inf)
            inter_log = b_col + mp_c[r:r + 1, :]
            m_t = jnp.maximum(inter_log, jnp.max(dlog, axis=-1, keepdims=True))
            dw_scr[slot, h] = jnp.exp2(dlog - m_t)
            iw_scr[slot, h] = jnp.exp2(inter_log - m_t)
            em_scr[slot, h] = jnp.exp2(-m_t)
            wc_scr[slot, h] = jnp.exp2(wl_col - mn_c[r:r + 1, :])

    c_scr[...] = jnp.zeros_like(c_scr)
    ones = jnp.ones((L, dh), BF16)

    def chunk(c, slot):
        ts = pl.ds(_aligned(c * L, L), L)
        for h in range(H):
            hs = slice(h * dh, (h + 1) * dh)
            q = q_ref[0, ts, hs]
            k = k_ref[0, ts, hs]
            v_aug = jnp.concatenate([v_ref[0, ts, hs], ones], axis=1)
            c_aug = c_scr[h]
            iw = iw_scr[slot, h]
            s = _dot_nt(q, k) * dw_scr[slot, h]
            qc = _bdot(q, c_aug.astype(BF16))
            sv = _bdot(s.astype(BF16), v_aug)
            num = sv[:, :dh] + iw * qc[:, :dh]
            den = sv[:, dh:] + iw * qc[:, dh:]
            hv = num / jnp.maximum(jnp.abs(den), em_scr[slot, h])
            hn = _rms(hv, gmh_ref[:, hs])
            o_ref[0, ts, hs] = (_sigmoid(om_ref[0, ts, hs]) * hn).astype(o_ref.dtype)

            decay = rows_scr[5, pl.ds(c * 2 * H + H + h, 1), :]
            kw_t = (k.astype(F32) * wc_scr[slot, h]).T.astype(BF16)
            c_scr[h] = jnp.concatenate([decay, decay], axis=1) * c_aug + _bdot(kw_t, v_aug)

    unroll = M_UNROLL if nc % M_UNROLL == 0 else 2
    prep(0, 0)
    if unroll == nc:
        for c in range(nc):
            chunk(c, c % 2)
            if c + 1 < nc:
                prep(c + 1, (c + 1) % 2)
        return

    def body(i, carry):
        c = unroll * i
        for k in range(unroll):
            chunk(c + k, k % 2)
            prep(jnp.minimum(c + k + 1, nc - 1), (k + 1) % 2)
        return carry

    lax.fori_loop(0, nc // unroll, body, 0)


def _mlstm(qm, km, vm, om, gr, g_mhead):
    B, S, W = qm.shape
    L = M_CHUNK
    uo = np.concatenate([np.triu(np.ones((L, L), np.float32)), np.ones((L, L), np.float32)], axis=1)
    seq = lambda: pl.BlockSpec((1, S, W), lambda b: (b, 0, 0))
    rows = 2 * M_HEADS * (S // L)
    assert (S // L) % 2 == 0, "the chunk loop handles two chunks per iteration"
    per_head = lambda: pltpu.VMEM((2, M_HEADS, L, L), F32)
    return pl.pallas_call(
        _mlstm_kernel,
        grid=(B,),
        in_specs=[seq(), seq(), seq(), seq(),
                  pl.BlockSpec((1, rows, L), lambda b: (b, 0, 0)),
                  pl.BlockSpec((SUBLANES, W), lambda b: (0, 0)),
                  pl.BlockSpec((L, 2 * L), lambda b: (0, 0))],
        out_specs=seq(),
        out_shape=jax.ShapeDtypeStruct((B, S, W), BF16),
        scratch_shapes=[pltpu.VMEM((M_HEADS, M_HEAD_DIM, 2 * M_HEAD_DIM), F32),
                        pltpu.VMEM((6, rows, L), F32),
                        per_head(), per_head(), per_head(), per_head()],
        compiler_params=pltpu.CompilerParams(dimension_semantics=("parallel",),
                                             vmem_limit_bytes=VMEM_LIMIT),
        name="mlstm",
    )(qm, km, vm, om, gr, g_mhead, jnp.asarray(uo, BF16))


def _attn_kernel(q_ref, k_ref, v_ref, o_ref, m_scr, acc_scr, s_scr, p_scr, al_scr):
    S = q_ref.shape[1]
    t = min(ATT_T, S)
    nq = S // t
    nh = ATT_HEADS
    causal = (lax.broadcasted_iota(jnp.int32, (t, t), 1)
              <= lax.broadcasted_iota(jnp.int32, (t, t), 0))

    def tile(i):
        return slice(i * t, (i + 1) * t)

    def head(hh):
        return slice(hh * LANES, (hh + 1) * LANES)

    def scores(blk, slot):
        qi, kt = blk
        for hh in range(nh):
            s_scr[slot, hh] = _dot_nt(q_ref[0, tile(qi), head(hh)], k_ref[0, tile(kt), head(hh)])

    def softmax(blk, slot, diag):
        qi, _ = blk
        for hh in range(nh):
            mask = (lambda x: jnp.where(causal, x, -jnp.inf)) if diag else (lambda x: x)
            row_max = jnp.max(mask(s_scr[slot, hh]), axis=-1, keepdims=True)
            if diag:
                m_new = jnp.broadcast_to(row_max, (t, LANES))
            else:
                m_old = m_scr[hh, tile(qi), :]
                m_new = jnp.maximum(m_old, row_max)
                al_scr[slot, hh] = jnp.exp2(m_old - m_new)
            m_scr[hh, tile(qi), :] = m_new
            p = jnp.exp2(mask(s_scr[slot, hh]) - jnp.concatenate([m_new] * (t // LANES), axis=1))
            p_scr[slot, hh // 2, (hh % 2) * t:(hh % 2 + 1) * t, :] = p.astype(BF16)

    ones = jnp.ones((t, LANES), BF16)

    def values(blk, slot, diag):
        qi, kt = blk
        for hh in range(nh):
            if hh % 2 == 0:
                pv = _bdot(p_scr[slot, hh // 2],
                           jnp.concatenate([v_ref[0, tile(kt), head(hh // 2)], ones], axis=1))
            part = pv[(hh % 2) * t:(hh % 2 + 1) * t]
            if diag:
                acc_scr[hh, tile(qi), :] = part
            else:
                alpha = al_scr[slot, hh]
                acc_scr[hh, tile(qi), :] = (jnp.concatenate([alpha, alpha], axis=1)
                                            * acc_scr[hh, tile(qi), :] + part)

    blocks = [(j, j) for j in range(nq)] + [(qi, kt) for qi in range(1, nq) for kt in range(qi)]
    n = len(blocks)
    for j in range(n + 2):
        slot = j % 2
        if j >= 2:
            values(blocks[j - 2], slot, j - 2 < nq)
        if 1 <= j <= n:
            softmax(blocks[j - 1], 1 - slot, j - 1 < nq)
        if j < n:
            scores(blocks[j], slot)

    lane = lax.broadcasted_iota(jnp.int32, (t, LANES), 1)
    for qi in range(nq):
        for pair in range(nh // 2):
            outs = [acc_scr[hh, tile(qi), :LANES] / acc_scr[hh, tile(qi), LANES:]
                    for hh in (2 * pair, 2 * pair + 1)]
            o_ref[0, tile(qi), head(pair)] = jnp.where(lane < A_V, outs[0], outs[1]).astype(o_ref.dtype)


def _attn(qa, ka, va):
    B, S, _ = qa.shape
    t = min(ATT_T, S)
    nh = ATT_HEADS
    group = lambda w: pl.BlockSpec((1, S, w), lambda b, p: (b, 0, p))
    return pl.pallas_call(
        _attn_kernel,
        grid=(B, A_HEADS // nh),
        in_specs=[group(nh * LANES), group(nh * LANES), group(nh * A_V)],
        out_specs=group(nh * A_V),
        out_shape=jax.ShapeDtypeStruct((B, S, A_WIDTH), BF16),
        scratch_shapes=[pltpu.VMEM((nh, S, LANES), F32), pltpu.VMEM((nh, S, 2 * LANES), F32),
                        pltpu.VMEM((2, nh, t, t), F32), pltpu.VMEM((2, nh // 2, 2 * t, t), BF16),
                        pltpu.VMEM((2, nh, t, LANES), F32)],
        compiler_params=pltpu.CompilerParams(dimension_semantics=("parallel", "parallel"),
                                             vmem_limit_bytes=VMEM_LIMIT),
        name="attn",
    )(qa, ka, va)


def _merge_kernel(x_ref, ada_ref, hm_ref, oa_ref, gmix_ref, wg_ref, wbm_ref, wba_ref, wo_ref, h_ref):
    x = x_ref[0]
    tm, D = x.shape
    gain = gmix_ref[...] * (1.0 + _ada_rows(ada_ref, 1))
    u = (_rms(x, gain) + _rows(_ada_rows(ada_ref, 0), tm)).astype(BF16)
    y = _sigmoid(_bdot(u, wg_ref[:, 0:D])) * _bdot(hm_ref[0], wbm_ref[...])
    y = y + _sigmoid(_bdot(u, wg_ref[:, D:2 * D])) * _bdot(oa_ref[0], wba_ref[...])
    h_ref[0] = x + _rows(_ada_rows(ada_ref, 2), tm) * _bdot(y.astype(BF16), wo_ref[...])


def _merge(x, ada8, hm, oa, g_mix, w_g, w_bm, w_ba, w_o):
    B, S, D = x.shape
    tm = min(MERGE_TM, S)
    const = lambda shape: pl.BlockSpec(shape, lambda b, i: (0,) * len(shape),
                                       pipeline_mode=pl.Buffered(1))
    tok = lambda w: pl.BlockSpec((1, tm, w), lambda b, i: (b, i, 0))
    return pl.pallas_call(
        _merge_kernel,
        grid=(B, S // tm),
        in_specs=[tok(D), pl.BlockSpec((1,) + ada8.shape[1:], lambda b, i: (b, 0, 0)), tok(M_WIDTH), tok(A_WIDTH),
                  const(g_mix.shape), const(w_g.shape), const(w_bm.shape), const(w_ba.shape),
                  const(w_o.shape)],
        out_specs=tok(D),
        out_shape=jax.ShapeDtypeStruct((B, S, D), F32),
        compiler_params=pltpu.CompilerParams(dimension_semantics=("parallel", "parallel"),
                                             vmem_limit_bytes=VMEM_LIMIT),
        name="merge",
    )(x, ada8, hm, oa, g_mix, w_g, w_bm, w_ba, w_o)


def _ffn_kernel(h_ref, ada_ref, gffn_ref, gfin_ref, wi_ref, wo_ref, o_ref):
    h = h_ref[0]
    tm = h.shape[0]
    gain = gffn_ref[...] * (1.0 + _ada_rows(ada_ref, 4))
    u = (_rms(h, gain) + _rows(_ada_rows(ada_ref, 3), tm)).astype(BF16)
    acc = None
    for lo, hi in FFN_CHUNKS:
        a = _silu(_bdot(u, wi_ref[:, lo:hi])) * _bdot(u, wi_ref[:, D_FF + lo:D_FF + hi])
        part = _bdot(a.astype(BF16), wo_ref[lo:hi, :])
        acc = part if acc is None else acc + part
    o_ref[0] = _rms(h + _rows(_ada_rows(ada_ref, 5), tm) * acc, gfin_ref[...])


def _ffn(h1, ada8, g_ffn, g_final, w_in, w_out):
    B, S, D = h1.shape
    tm = min(FFN_TM, S)
    tok = pl.BlockSpec((1, tm, D), lambda b, i: (b, i, 0))
    resident = lambda shape: pl.BlockSpec(shape, lambda b, i: (0, 0), pipeline_mode=pl.Buffered(1))
    return pl.pallas_call(
        _ffn_kernel,
        grid=(B, S // tm),
        in_specs=[tok,
                  pl.BlockSpec((1,) + ada8.shape[1:], lambda b, i: (b, 0, 0)),
                  resident(g_ffn.shape), resident(g_final.shape), resident(w_in.shape),
                  resident(w_out.shape)],
        out_specs=tok,
        out_shape=jax.ShapeDtypeStruct((B, S, D), F32),
        compiler_params=pltpu.CompilerParams(dimension_semantics=("parallel", "parallel"),
                                             vmem_limit_bytes=VMEM_LIMIT),
        name="ffn",
    )(h1, ada8, g_ffn, g_final, w_in, w_out)


def _pack_weights(w_in, w_uq, w_ukv):
    D = w_in.shape[0]
    o_gate = 4 * M_WIDTH
    o_qlat = o_gate + 2 * M_HEADS
    o_kr = o_qlat + A_Q_RANK + A_KV_RANK
    o_gm = o_kr + A_ROPE
    zeros = lambda n: jnp.zeros((D, n), w_in.dtype)
    half = A_ROPE // 2
    misc = jnp.concatenate([zeros(A_NOPE), w_in[:, o_kr:o_gm], w_in[:, o_kr:o_kr + half],
                            zeros(LANES - A_QK - half)], axis=1)
    w_b = jnp.concatenate([w_in[:, :o_gate], w_in[:, o_qlat:o_kr], misc], axis=1).astype(BF16)
    w_gt = w_in[:, o_gate:o_qlat].T.astype(BF16)
    w_g = w_in[:, o_gm:].astype(BF16)
    wuq = w_uq.reshape(A_Q_RANK, A_HEADS, A_QK)
    wuq = jnp.concatenate([wuq, wuq[:, :, A_NOPE:A_NOPE + half],
                           jnp.zeros((A_Q_RANK, A_HEADS, LANES - A_QK - half), w_uq.dtype)], axis=2)
    wuq = wuq.reshape(A_Q_RANK, A_HEADS * LANES).astype(BF16)
    wkv = w_ukv.reshape(A_KV_RANK, A_HEADS, A_NOPE + A_V)
    wk = jnp.pad(wkv[:, :, :A_NOPE], ((0, 0), (0, 0), (0, LANES - A_NOPE)))
    wk = wk.reshape(A_KV_RANK, A_HEADS * LANES).astype(BF16)
    wv = wkv[:, :, A_NOPE:].reshape(A_KV_RANK, A_WIDTH).astype(BF16)
    return w_b, w_gt, w_g, wuq, wk, wv


def _rope_rows(groups):
    inv_freq = ROPE_THETA ** (-jnp.arange(0, A_ROPE, 2, dtype=F32) / A_ROPE)
    half = A_ROPE // 2
    pair = jnp.concatenate([inv_freq, inv_freq])
    rows = []
    for g in range(groups):
        row = jnp.zeros((LANES,), F32).at[A_ROPE * g:A_ROPE * (g + 1)].set(pair)
        rows.append(jnp.broadcast_to(row, (SUBLANES, LANES)))
    sign = jnp.concatenate([-jnp.ones((half,), F32), jnp.ones((half,), F32)])
    sgn = jnp.broadcast_to(jnp.tile(sign, LANES // A_ROPE), (SUBLANES, LANES))
    return jnp.concatenate(rows, axis=0), sgn


def _rep8(v):
    v = v.reshape(-1, v.shape[-1])
    return jnp.repeat(v, SUBLANES, axis=0)


def kernel(x, c, positions, w_ada, b_ada, g_mix, w_in, conv_w, conv_b, b_igate, b_fgate, g_mhead,
           g_q_lat, w_uq, g_kv_lat, w_ukv, w_branch_m, w_branch_a, w_out, g_ffn, w_ffn_in,
           w_ffn_out, g_final):
    B, S, D = x.shape
    assert w_ada.shape[0] == 1, "the final rmsnorm is fused into the (single) layer's FFN kernel"
    invf, sgn = _rope_rows(LANES // A_ROPE)
    pos3 = positions.reshape(B, S // LANES, LANES)
    l = 0
    ada = _ada(c, w_ada[l], b_ada[l])
    ada8 = jnp.repeat(ada.reshape(B, 6, 1, D), SUBLANES, axis=2).reshape(B, 6 * SUBLANES, D)
    w_b, w_gt, w_g, wuq, wk, wv = _pack_weights(w_in[l], w_uq[l], w_ukv[l])
    b_col = jnp.concatenate([b_igate[l], b_fgate[l]]).astype(F32).reshape(2 * M_HEADS, 1)
    gmix = _rep8(g_mix[l])
    qm, km, vm, om, gr, qa, ka, va = _mix_in(
        x, ada8, pos3, gmix, w_b, w_gt, _rep8(conv_w[l]), _rep8(conv_b[l]), b_col,
        _rep8(g_q_lat[l]), _rep8(g_kv_lat[l]), wuq, wk, wv, invf, sgn)
    hm = _mlstm(qm, km, vm, om, gr, _rep8(g_mhead[l]))
    oa = _attn(qa, ka, va)
    h1 = _merge(x, ada8, hm, oa, gmix, w_g, w_branch_m[l].astype(BF16),
                w_branch_a[l].astype(BF16), w_out[l].astype(BF16))
    return _ffn(h1, ada8, _rep8(g_ffn[l]), _rep8(g_final), w_ffn_in[l].astype(BF16),
                w_ffn_out[l].astype(BF16))
```
